```python
import math
import jax
import jax.numpy as jnp
from jax import lax
import numpy as np

D_MODEL = 1024
BATCH = 4
SEQ = 8192
DEPTH = 4

GRID_W = 64
CTX_LEN = 256
ROPE_BASE = 10000.0

A_HEADS = 4
A_DQK = D_MODEL // (8 * A_HEADS)
A_DV = 2 * A_DQK
A_WIDTH = A_HEADS * A_DV
A_QK = A_HEADS * 2 * A_DQK
Q_BLOCK = 128

B_HEADS = 4
B_DK = D_MODEL // (4 * B_HEADS)
B_DV = D_MODEL // (4 * B_HEADS)
B_WIDTH = B_HEADS * B_DV
B_QK = B_HEADS * B_DK
RET_CHUNK = 128
RET_LOG_GAMMA = tuple(math.log(1.0 - 2.0 ** (-5 - h)) for h in range(B_HEADS))

C_GROUPS = 4
C_WIDTH = D_MODEL // 4
C_GW = C_WIDTH // C_GROUPS
POOL_WINDOWS = (2, 4, 8, 16)

D_WIDTH = D_MODEL // 4
CONV_K = 31

MIX_WIDTH = A_WIDTH + B_WIDTH + C_WIDTH + D_WIDTH
D_FF = ((8 * D_MODEL + 3 * 256 - 1) // (3 * 256)) * 256

COL_SIZES = (A_QK, A_WIDTH, B_QK, B_WIDTH, A_QK, B_QK, B_WIDTH, B_WIDTH, C_WIDTH, 2 * D_WIDTH)
CTX_COLS = A_QK + A_WIDTH + B_QK + B_WIDTH
IN_COLS = sum(COL_SIZES)

kernel_name = 'hybrid_diffusion_parallel_groups'


def _rms_norm(x, g, eps=1e-6):
    xf = x.astype(jnp.float32)
    y = xf * lax.rsqrt(jnp.mean(xf * xf, axis=-1, keepdims=True) + eps)
    return (y * g.astype(jnp.float32)).astype(x.dtype)


def _layer_norm(x, g, b, eps=1e-5):
    xf = x.astype(jnp.float32)
    mu = jnp.mean(xf, axis=-1, keepdims=True)
    var = jnp.mean(jnp.square(xf - mu), axis=-1, keepdims=True)
    y = (xf - mu) * lax.rsqrt(var + eps) * g.astype(jnp.float32) + b.astype(jnp.float32)
    return y.astype(x.dtype)


def _modulate(h, shift, scale):
    return h * (1 + scale) + shift


def _split_cols(u):
    parts, start = [], 0
    for size in COL_SIZES:
        if start >= u.shape[-1]:
            break
        parts.append(u[..., start:start + size])
        start += size
    return parts


def _axial_rope(row, col, d, dtype):
    n_freq = d // 4
    inv = ROPE_BASE ** (-jnp.arange(n_freq, dtype=jnp.float32) / n_freq)
    ang = jnp.concatenate([row.astype(jnp.float32)[:, None] * inv,
                           col.astype(jnp.float32)[:, None] * inv], axis=-1)
    return jnp.cos(ang).astype(dtype), jnp.sin(ang).astype(dtype)


def _apply_rope(x, cos, sin):
    x1, x2 = jnp.split(x, 2, axis=-1)
    return jnp.concatenate([x1 * cos - x2 * sin, x1 * sin + x2 * cos], axis=-1)


def _heads_qk(u):
    b, t, _ = u.shape
    return u.reshape(b, t, A_HEADS, 2, A_DQK).transpose(0, 2, 3, 1, 4)


def _heads_v(u, n_heads):
    b, t, w = u.shape
    return u.reshape(b, t, n_heads, w // n_heads).transpose(0, 2, 1, 3)


def _diff_attend(q, k, v, lam):
    s = jnp.einsum('bhcqd,bhckd->bhcqk', q, k, preferred_element_type=jnp.float32) * (A_DQK ** -0.5)
    p = jax.nn.softmax(s, axis=-1)
    w = p[:, :, 0] - lam * p[:, :, 1]
    return jnp.einsum('bhqk,bhkd->bhqd', w.astype(v.dtype), v)


def _diff_attn_blocks(q, k_all, v_all, lam):
    b, h, _, t, d = q.shape
    nb = t // Q_BLOCK
    qb = q.reshape(b, h, 2, nb, Q_BLOCK, d).transpose(3, 0, 1, 2, 4, 5)
    out = lax.map(lambda qi: _diff_attend(qi, k_all, v_all, lam), qb)
    return out.transpose(1, 2, 0, 3, 4).reshape(b, h, t, A_DV)


def _diff_head_out(o, g, lam_init):
    b, h, t, dv = o.shape
    o = _rms_norm(o.transpose(0, 2, 1, 3), g, eps=1e-5) * (1.0 - lam_init)
    return o.reshape(b, t, h * dv)


def _chunk(x):
    b, h, t, d = x.shape
    return x.reshape(b, h, t // RET_CHUNK, RET_CHUNK, d)


def _ret_scan(kc, vc, s0):
    c = kc.shape[3]
    lg = jnp.asarray(RET_LOG_GAMMA, jnp.float32)
    j = jnp.arange(c, dtype=jnp.float32)
    w_end = jnp.exp(lg[:, None] * (c - 1.0 - j))
    kv = jnp.einsum('bhncd,hc,bhnce->nbhde', kc.astype(jnp.float32), w_end,
                    vc.astype(jnp.float32))
    chunk_decay = jnp.exp(lg * c)[None, :, None, None]

    def step(s, kv_n):
        return chunk_decay * s + kv_n, s

    s_fin, s_prev = lax.scan(step, s0, kv)
    return s_prev, s_fin


def _ret_out(qc, kc, vc, s_prev):
    c = qc.shape[3]
    lg = jnp.asarray(RET_LOG_GAMMA, jnp.float32)
    j = jnp.arange(c, dtype=jnp.float32)
    dist = j[:, None] - j[None, :]
    dmat = jnp.where(dist >= 0, jnp.exp(lg[:, None, None] * jnp.maximum(dist, 0.0)), 0.0)
    qf, kf, vf = qc.astype(jnp.float32), kc.astype(jnp.float32), vc.astype(jnp.float32)
    scores = jnp.einsum('bhncd,bhnmd->bhncm', qf, kf) * dmat[None, :, None]
    y = jnp.einsum('bhncm,bhnme->bhnce', scores, vf)
    cross_decay = jnp.exp(lg[:, None] * (j + 1.0))
    y = y + jnp.einsum('bhncd,nbhde->bhnce', qf, s_prev) * cross_decay[None, :, None, :, None]
    b, h, n, _, dv = y.shape
    return y.reshape(b, h, n * c, dv)


def _retention(q, k, v, s0):
    qc, kc, vc = _chunk(q), _chunk(k), _chunk(v)
    s_prev, s_fin = _ret_scan(kc, vc, s0)
    return _ret_out(qc, kc, vc, s_prev), s_fin


def _head_layer_norm(y, dtype, eps=1e-6):
    b, h, t, d = y.shape
    mu = jnp.mean(y, axis=-1, keepdims=True)
    var = jnp.mean(jnp.square(y - mu), axis=-1, keepdims=True)
    yn = (y - mu) * lax.rsqrt(var + eps)
    return yn.transpose(0, 2, 1, 3).reshape(b, t, h * d).astype(dtype)


def _pool_mixer(u, w_pool_l, s_pool_l):
    b, t, cw = u.shape
    uf = u.astype(jnp.float32)
    csum = jnp.concatenate([jnp.zeros((b, 1, cw), jnp.float32), jnp.cumsum(uf, axis=1)], axis=1)
    pos = jnp.arange(t)
    parts = []
    for g, w in enumerate(POOL_WINDOWS):
        lo = jnp.clip(pos - w // 2, 0, t)
        hi = jnp.clip(pos + w - w // 2, 0, t)
        cs = csum[..., g * C_GW:(g + 1) * C_GW]
        mean = (cs[:, hi] - cs[:, lo]) / (hi - lo).astype(jnp.float32)[None, :, None]
        parts.append(mean - uf[..., g * C_GW:(g + 1) * C_GW])
    d = jnp.stack(parts, axis=2)
    y = jnp.einsum('btgc,gce->btge', d, w_pool_l.astype(jnp.float32)).reshape(b, t, cw)
    return (y * s_pool_l.astype(jnp.float32)).astype(u.dtype)


def _conformer_conv(u, conv_w_l, conv_b_l, ln_g_l, ln_b_l, w_pw2_l):
    a, g = jnp.split(u, 2, axis=-1)
    h = a * jax.nn.sigmoid(g)
    h = lax.conv_general_dilated(h, conv_w_l, window_strides=(1,),
                                 padding=[(CONV_K // 2, CONV_K // 2)],
                                 dimension_numbers=('NWC', 'WIO', 'NWC'),
                                 feature_group_count=D_WIDTH) + conv_b_l
    h = _layer_norm(h, ln_g_l, ln_b_l)
    return jax.nn.silu(h) @ w_pw2_l


def _swiglu(h, wg, wu, wd):
    return (jax.nn.silu(h @ wg) * (h @ wu)) @ wd


def _mixing(u_lat, u_ctx, w_in_l, w_out_l, lam, lam_init, g_subln_l, w_pool_l, s_pool_l,
            conv_w_l, conv_b_l, conv_ln_g_l, conv_ln_b_l, w_conv_out_l, rope_a, rope_b, last):
    dtype = u_lat.dtype
    bsz = u_lat.shape[0]
    p_lat = u_lat @ w_in_l
    p_ctx = u_ctx @ (w_in_l[:, :CTX_COLS] if last else w_in_l)
    kA_l, vA_l, kB_l, vB_l, qA_l, qB_l, gF_l, gB_l, pool_l, glu_l = _split_cols(p_lat)
    pc = _split_cols(p_ctx)

    qA = _apply_rope(_heads_qk(qA_l), *rope_a)
    kA_ctx = _heads_qk(pc[0])
    vA_ctx = _heads_v(pc[1], A_HEADS)
    kA_all = jnp.concatenate([kA_ctx, _apply_rope(_heads_qk(kA_l), *rope_a)], axis=3)
    vA_all = jnp.concatenate([vA_ctx, _heads_v(vA_l, A_HEADS)], axis=2)
    yA_lat = _diff_head_out(_diff_attn_blocks(qA, kA_all, vA_all, lam), g_subln_l, lam_init)

    k_scale = B_DK ** -0.5
    qB = _apply_rope(_heads_v(qB_l, B_HEADS), *rope_b)
    kB = _apply_rope(_heads_v(kB_l, B_HEADS), *rope_b) * k_scale
    vB = _heads_v(vB_l, B_HEADS)
    kB_ctx = _heads_v(pc[2], B_HEADS) * k_scale
    vB_ctx = _heads_v(pc[3], B_HEADS)
    s0 = jnp.zeros((bsz, B_HEADS, B_DK, B_DV), jnp.float32)
    flip = lambda t: jnp.flip(t, axis=2)
    if last:
        s_cf = _ret_scan(_chunk(kB_ctx), _chunk(vB_ctx), s0)[1]
        s_cb = _ret_scan(_chunk(flip(kB_ctx)), _chunk(flip(vB_ctx)), s0)[1]
    else:
        qB_ctx = _heads_v(pc[5], B_HEADS)
        yf_c, s_cf = _retention(qB_ctx, kB_ctx, vB_ctx, s0)
        yb_c, s_cb = _retention(flip(qB_ctx), flip(kB_ctx), flip(vB_ctx), s0)
        yB_ctx = (jax.nn.silu(pc[6]) * _head_layer_norm(yf_c, dtype)
                  + jax.nn.silu(pc[7]) * _head_layer_norm(flip(yb_c), dtype))
    yf = _retention(qB, kB, vB, s_cf)[0]
    yb = flip(_retention(flip(qB), flip(kB), flip(vB), s_cb)[0])
    yB_lat = jax.nn.silu(gF_l) * _head_layer_norm(yf, dtype) + jax.nn.silu(gB_l) * _head_layer_norm(yb, dtype)

    yC_lat = _pool_mixer(pool_l, w_pool_l, s_pool_l)
    yD_lat = _conformer_conv(glu_l, conv_w_l, conv_b_l, conv_ln_g_l, conv_ln_b_l, w_conv_out_l)

    y_lat = jnp.concatenate([yA_lat, yB_lat, yC_lat, yD_lat], axis=-1) @ w_out_l
    if last:
        return y_lat, None
    yA_ctx = _diff_head_out(_diff_attend(_heads_qk(pc[4]), kA_ctx, vA_ctx, lam), g_subln_l, lam_init)
    yC_ctx = _pool_mixer(pc[8], w_pool_l, s_pool_l)
    yD_ctx = _conformer_conv(pc[9], conv_w_l, conv_b_l, conv_ln_g_l, conv_ln_b_l, w_conv_out_l)
    y_ctx = jnp.concatenate([yA_ctx, yB_ctx, yC_ctx, yD_ctx], axis=-1) @ w_out_l
    return y_lat, y_ctx


def setup_inputs(seed: int = 0) -> dict:
    key = jax.random.key(seed)
    ks = jax.random.split(key, 28)
    L, D = DEPTH, D_MODEL

    def nrm(k, shape, s):
        return jax.random.normal(k, shape, jnp.float32) * s

    def gain(k, shape):
        return 1.0 + 0.1 * jax.random.normal(k, shape, jnp.float32)

    return {
        'x': nrm(ks[0], (BATCH, SEQ, D), 1.0),
        'c': nrm(ks[1], (BATCH, D), 1.0),
        'ctx': nrm(ks[2], (BATCH, CTX_LEN, D), 1.0),
        'c_ctx': nrm(ks[3], (D,), 1.0),
        'w_ada': nrm(ks[4], (L, D, 6 * D), 0.5 * D ** -0.5),
        'b_ada': nrm(ks[5], (L, 6 * D), 0.02),
        'g_pre_mix': gain(ks[6], (L, D)),
        'g_post_mix': gain(ks[7], (L, D)),
        'g_pre_ffn': gain(ks[8], (L, D)),
        'g_post_ffn': gain(ks[9], (L, D)),
        'w_in': nrm(ks[10], (L, D, IN_COLS), D ** -0.5),
        'w_out': nrm(ks[11], (L, MIX_WIDTH, D), MIX_WIDTH ** -0.5),
        'lam_q1': nrm(ks[12], (L, A_DQK), 0.1),
        'lam_k1': nrm(ks[13], (L, A_DQK), 0.1),
        'lam_q2': nrm(ks[14], (L, A_DQK), 0.1),
        'lam_k2': nrm(ks[15], (L, A_DQK), 0.1),
        'g_subln': gain(ks[16], (L, A_DV)),
        'w_pool': nrm(ks[17], (L, C_GROUPS, C_GW, C_GW), C_GW ** -0.5),
        's_pool': 0.5 + 0.1 * jax.random.normal(ks[18], (L, C_WIDTH), jnp.float32),
        'conv_w': nrm(ks[19], (L, CONV_K, 1, D_WIDTH), CONV_K ** -0.5),
        'conv_b': nrm(ks[20], (L, D_WIDTH), 0.02),
        'conv_ln_g': gain(ks[21], (L, D_WIDTH)),
        'conv_ln_b': nrm(ks[22], (L, D_WIDTH), 0.02),
        'w_conv_out': nrm(ks[23], (L, D_WIDTH, D_WIDTH), D_WIDTH ** -0.5),
        'w_ffn_gate': nrm(ks[24], (L, D, D_FF), D ** -0.5),
        'w_ffn_up': nrm(ks[25], (L, D, D_FF), D ** -0.5),
        'w_ffn_down': nrm(ks[26], (L, D_FF, D), D_FF ** -0.5),
    }


def reference(x, c, ctx, c_ctx, w_ada, b_ada, g_pre_mix, g_post_mix, g_pre_ffn, g_post_ffn,
              w_in, w_out, lam_q1, lam_k1, lam_q2, lam_k2, g_subln, w_pool, s_pool,
              conv_w, conv_b, conv_ln_g, conv_ln_b, w_conv_out, w_ffn_gate, w_ffn_up, w_ffn_down):
    t_len = x.shape[1]
    ROWS = t_len // GRID_W
    row = jnp.repeat(jnp.arange(ROWS, dtype=jnp.int32), GRID_W)
    col = jnp.tile(jnp.arange(GRID_W, dtype=jnp.int32), ROWS)
    rope_a = _axial_rope(row, col, A_DQK, x.dtype)
    rope_b = _axial_rope(row, col, B_DK, x.dtype)
    sc = jax.nn.silu(c)
    scc = jax.nn.silu(c_ctx)
    h_lat, h_ctx = x, ctx
    for l in range(DEPTH):
        last = l == DEPTH - 1
        mod_lat = (sc @ w_ada[l] + b_ada[l])[:, None, :]
        mod_ctx = (scc @ w_ada[l] + b_ada[l])[None, None, :]
        sm, cm, gm, sf, cf, gf = jnp.split(mod_lat, 6, axis=-1)
        sm_c, cm_c, gm_c, sf_c, cf_c, gf_c = jnp.split(mod_ctx, 6, axis=-1)
        lam_init = 0.8 - 0.6 * math.exp(-0.3 * l)
        lam = (jnp.exp(jnp.sum(lam_q1[l] * lam_k1[l]).astype(jnp.float32))
               - jnp.exp(jnp.sum(lam_q2[l] * lam_k2[l]).astype(jnp.float32)) + lam_init)

        u_lat = _modulate(_rms_norm(h_lat, g_pre_mix[l]), sm, cm)
        u_ctx = _modulate(_rms_norm(h_ctx, g_pre_mix[l]), sm_c, cm_c)
        y_lat, y_ctx = _mixing(u_lat, u_ctx, w_in[l], w_out[l], lam, lam_init, g_subln[l],
                               w_pool[l], s_pool[l], conv_w[l], conv_b[l], conv_ln_g[l],
                               conv_ln_b[l], w_conv_out[l], rope_a, rope_b, last)
        h_lat = h_lat + gm * _rms_norm(y_lat, g_post_mix[l])
        f_lat = _swiglu(_modulate(_rms_norm(h_lat, g_pre_ffn[l]), sf, cf),
                        w_ffn_gate[l], w_ffn_up[l], w_ffn_down[l])
        h_lat = h_lat + gf * _rms_norm(f_lat, g_post_ffn[l])
        if not last:
            h_ctx = h_ctx + gm_c * _rms_norm(y_ctx, g_post_mix[l])
            f_ctx = _swiglu(_modulate(_rms_norm(h_ctx, g_pre_ffn[l]), sf_c, cf_c),
                            w_ffn_gate[l], w_ffn_up[l], w_ffn_down[l])
            h_ctx = h_ctx + gf_c * _rms_norm(f_ctx, g_post_ffn[l])
    return h_lat
```

```python
import functools
import math

import jax
import jax.numpy as jnp
from jax import lax
from jax.experimental import pallas as pl
from jax.experimental.pallas import tpu as pltpu

F32 = jnp.float32
BF16 = jnp.bfloat16

GRID_W = 64
ROPE_BASE = 10000.0
A_HEADS = 4
B_HEADS = 4
POOL_WINDOWS = (2, 4, 8, 16)
CONV_K = 31

LANES = 128
MXU_TILE = 256
ROW_TILE = 256
HALO = 16
VMEM_LIMIT = 56 * 1024 * 1024


def _cparams(sem):
    return pltpu.CompilerParams(dimension_semantics=sem, vmem_limit_bytes=VMEM_LIMIT)


def _rms(x, eps):
    return x * lax.rsqrt(jnp.mean(x * x, axis=-1, keepdims=True) + eps)


def _dot(a, b):
    return jnp.dot(a, b, preferred_element_type=F32)


def _mod_kernel(c_ref, w_ref, b_ref, o_ref):
    a = jax.nn.silu(c_ref[...])
    o_ref[...] = jnp.dot(a, w_ref[...], preferred_element_type=F32,
                         precision=lax.Precision.HIGHEST) + b_ref[...]


def _modulation(cond, w_ada, b_ada):
    n_layers, d, n6 = w_ada.shape
    tn = 2048
    return pl.pallas_call(
        _mod_kernel,
        grid=(n_layers, n6 // tn),
        in_specs=[pl.BlockSpec((8, d), lambda l, j: (0, 0)),
                  pl.BlockSpec((None, d, tn), lambda l, j: (l, 0, j)),
                  pl.BlockSpec((None, 1, tn), lambda l, j: (l, 0, j))],
        out_specs=pl.BlockSpec((None, 8, tn), lambda l, j: (l, 0, j)),
        out_shape=jax.ShapeDtypeStruct((n_layers, 8, n6), F32),
        compiler_params=_cparams(("parallel", "parallel")),
        name="adaln_mod",
    )(cond, w_ada, b_ada.reshape(n_layers, 1, n6))


def _inproj_kernel(a_scale, b_scale, h_ref, mod_ref, g_ref, wn_ref, wt_ref, tn_ref, tt_ref,
                   pn_ref, pt_ref):
    mod = mod_ref[...]
    u = _rms(h_ref[...], 1e-6) * g_ref[...]
    ub = (u * (1.0 + mod[1:2]) + mod[0:1]).astype(BF16)

    def nat(c0, c1):
        return _dot(ub, wn_ref[:, c0:c1])

    def tr(r0, r1):
        return lax.dot_general(wt_ref[r0:r1, :], ub, (((1,), (1,)), ((), ())),
                               preferred_element_type=F32)

    pn_ref[:, 0:512] = nat(0, 512).astype(BF16)
    ka = nat(512, 768)
    x1, x2 = ka[:, :LANES], ka[:, LANES:]
    cs, sn = tn_ref[:, 0:128], tn_ref[:, 128:256]
    pn_ref[:, 512:640] = (x1 * cs - x2 * sn).astype(BF16)
    pn_ref[:, 640:768] = (x1 * sn + x2 * cs).astype(BF16)
    qb = nat(768, 1024)
    x1, x2 = qb[:, :LANES], qb[:, LANES:]
    cs, sn = tn_ref[:, 256:384], tn_ref[:, 384:512]
    pn_ref[:, 768:896] = (x1 * cs - x2 * sn).astype(BF16)
    pn_ref[:, 896:1024] = (x1 * sn + x2 * cs).astype(BF16)
    pn_ref[:, 1024:2048] = nat(1024, 2048).astype(BF16)

    qa = tr(0, 256) * a_scale
    x1, x2 = qa[:LANES], qa[LANES:]
    cs, sn = tt_ref[0:128, :], tt_ref[128:256, :]
    pt_ref[0:128, :] = (x1 * cs - x2 * sn).astype(BF16)
    pt_ref[128:256, :] = (x1 * sn + x2 * cs).astype(BF16)
    kb = tr(256, 512) * b_scale
    x1, x2 = kb[:LANES], kb[LANES:]
    cs, sn = tt_ref[256:384, :], tt_ref[384:512, :]
    pt_ref[256:384, :] = (x1 * cs - x2 * sn).astype(BF16)
    pt_ref[384:512, :] = (x1 * sn + x2 * cs).astype(BF16)
    pt_ref[512:768, :] = tr(512, 768).astype(BF16)


def _inproj(h, mod, g_pre, w_nat, w_tr, tab_n, tab_t, nc, a_scale, b_scale):
    bsz, s, d = h.shape
    nt = s // ROW_TILE
    nn, ntr = w_nat.shape[1], w_tr.shape[0]
    msel = lambda b, i: (jnp.where(i < nc, bsz, b), 0, 0)
    return pl.pallas_call(
        functools.partial(_inproj_kernel, a_scale, b_scale),
        grid=(bsz, nt),
        in_specs=[pl.BlockSpec((None, ROW_TILE, d), lambda b, i: (b, i, 0)),
                  pl.BlockSpec((None, 6, d), msel),
                  pl.BlockSpec((1, d), lambda b, i: (0, 0)),
                  pl.BlockSpec((d, nn), lambda b, i: (0, 0)),
                  pl.BlockSpec((ntr, d), lambda b, i: (0, 0)),
                  pl.BlockSpec((ROW_TILE, 4 * LANES), lambda b, i: (i, 0)),
                  pl.BlockSpec((4 * LANES, ROW_TILE), lambda b, i: (0, i))],
        out_specs=[pl.BlockSpec((None, ROW_TILE, nn), lambda b, i: (b, i, 0)),
                   pl.BlockSpec((None, ntr, ROW_TILE), lambda b, i: (b, 0, i))],
        out_shape=[jax.ShapeDtypeStruct((bsz, s, nn), BF16),
                   jax.ShapeDtypeStruct((bsz, ntr, s), BF16)],
        compiler_params=_cparams(("parallel", "parallel")),
        name="in_proj",
    )(h, mod, g_pre, w_nat, w_tr, tab_n, tab_t)


def _attn_kernel(nc, nt, lam_init, qt_ref, k_ref, vt_ref, lamv_ref, g_ref, o_ref,
                 qz_ref, m_ref, l_ref, acc_ref):
    qi = pl.program_id(1)
    ki = pl.program_id(2)
    n_groups = 2 * A_HEADS
    dv = vt_ref.shape[0] // A_HEADS

    @pl.when(ki == 0)
    def _init():
        qt = qt_ref[...]
        rg = (lax.broadcasted_iota(jnp.int32, qt.shape, 0) & (LANES - 1)) >> 4
        for g in range(n_groups):
            qz_ref[g] = jnp.where(rg == g, qt, jnp.zeros_like(qt))
        m_ref[...] = jnp.full(m_ref.shape, -1e30, F32)
        l_ref[...] = jnp.zeros(l_ref.shape, F32)
        acc_ref[...] = jnp.zeros(acc_ref.shape, F32)

    @pl.when((qi >= nc) | (ki < nc))
    def _step():
        k = k_ref[...]
        for g in range(n_groups):
            hd = g // 2
            s = _dot(k, qz_ref[g])
            m_old = m_ref[g]
            m_new = jnp.maximum(m_old, jnp.max(s, axis=0, keepdims=True))
            alpha = jnp.exp(m_old - m_new)
            p = jnp.exp(s - m_new)
            l_ref[g] = alpha * l_ref[g] + jnp.sum(p, axis=0, keepdims=True)
            pv = _dot(vt_ref[hd * dv:(hd + 1) * dv, :], p.astype(BF16))
            acc_ref[g] = alpha * acc_ref[g] + pv
            m_ref[g] = m_new

    @pl.when(ki == nt - 1)
    def _fin():
        lv = lamv_ref[...]
        a1 = jnp.sum(lv[0:1] * lv[1:2], axis=-1, keepdims=True)
        a2 = jnp.sum(lv[2:3] * lv[3:4], axis=-1, keepdims=True)
        lam = jnp.exp(a1) - jnp.exp(a2) + lam_init
        outs = []
        for hd in range(A_HEADS):
            o = acc_ref[2 * hd] / l_ref[2 * hd] - lam * (acc_ref[2 * hd + 1] / l_ref[2 * hd + 1])
            ms = jnp.mean(o * o, axis=0, keepdims=True)
            outs.append(o * lax.rsqrt(ms + 1e-5) * g_ref[...] * (1.0 - lam_init))
        o_ref[...] = jnp.concatenate(outs, axis=0).T.astype(o_ref.dtype)


def _attention(pn, pt, lamv, g_col, nc, lam_init):
    bsz, s, _ = pn.shape
    nt = s // ROW_TILE
    w = MXU_TILE
    kidx = lambda qi, ki: jnp.where(qi < nc, jnp.minimum(ki, nc - 1), ki)
    return pl.pallas_call(
        functools.partial(_attn_kernel, nc, nt, lam_init),
        grid=(bsz, nt, nt),
        in_specs=[pl.BlockSpec((None, w, ROW_TILE), lambda b, qi, ki: (b, 0, qi)),
                  pl.BlockSpec((None, ROW_TILE, w), lambda b, qi, ki: (b, kidx(qi, ki), 2)),
                  pl.BlockSpec((None, w, ROW_TILE), lambda b, qi, ki: (b, 2, kidx(qi, ki))),
                  pl.BlockSpec(lamv.shape, lambda b, qi, ki: (0, 0)),
                  pl.BlockSpec(g_col.shape, lambda b, qi, ki: (0, 0))],
        out_specs=pl.BlockSpec((None, ROW_TILE, w), lambda b, qi, ki: (b, qi, 0)),
        out_shape=jax.ShapeDtypeStruct((bsz, s, w), BF16),
        scratch_shapes=[pltpu.VMEM((2 * A_HEADS, w, ROW_TILE), BF16),
                        pltpu.VMEM((2 * A_HEADS, 1, ROW_TILE), F32),
                        pltpu.VMEM((2 * A_HEADS, 1, ROW_TILE), F32),
                        pltpu.VMEM((2 * A_HEADS, w // A_HEADS, ROW_TILE), F32)],
        compiler_params=_cparams(("parallel", "parallel", "arbitrary")),
        name="diff_attn",
    )(pt, pn, pt, lamv, g_col)


def _group_norm(y, gmat, eps):
    def gmean(z):
        zh = z.astype(BF16)
        zl = (z - zh.astype(F32)).astype(BF16)
        return _dot(zh, gmat) + _dot(zl, gmat)

    d = y - gmean(y)
    return d * lax.rsqrt(gmean(d * d) + eps)


def _ret_kernel(qf_ref, ktf_ref, vf_ref, qb_ref, ktb_ref, vb_ref, dm_ref, wend_ref, cross_ref,
                decs_ref, bd_ref, gm_ref, yf_ref, yb_ref, s_ref):
    i = pl.program_id(1)

    @pl.when(i == 0)
    def _init():
        s_ref[...] = jnp.zeros(s_ref.shape, F32)

    def one_dir(dr, q_ref, kt_ref, v_ref, y_ref):
        q = q_ref[...]
        kt = kt_ref[...]
        v = v_ref[...]
        rh = (lax.broadcasted_iota(jnp.int32, kt.shape, 0) & (LANES - 1)) >> 5
        ch = lax.broadcasted_iota(jnp.int32, (q.shape[0], v.shape[1]), 1) >> 6
        y = jnp.zeros((q.shape[0], v.shape[1]), F32)
        for hd in range(B_HEADS):
            ktz = jnp.where(rh == hd, kt, jnp.zeros_like(kt))
            p = (_dot(q, ktz) * dm_ref[dr, hd]).astype(BF16)
            y = jnp.where(ch == hd, _dot(p, v), y)
        s_old = s_ref[dr]
        y = y + _dot(q, s_old.astype(BF16)) * cross_ref[dr]
        kw = (kt.astype(F32) * wend_ref[dr]).astype(BF16)
        s_ref[dr] = decs_ref[...] * s_old + bd_ref[...] * _dot(kw, v)
        y_ref[...] = _group_norm(y, gm_ref[...], 1e-6).astype(y_ref.dtype)

    one_dir(0, qf_ref, ktf_ref, vf_ref, yf_ref)
    one_dir(1, qb_ref, ktb_ref, vb_ref, yb_ref)


def _retention(pn, pt, tabs, nc):
    bsz, s, _ = pn.shape
    c = ROW_TILE
    nt = s // c
    w = MXU_TILE
    dmask, wend, cross, decs, bdm, gmat = tabs
    cb = lambda i: jnp.where(i < nc, nc - 1 - i, nt - 1 - (i - nc))
    const = lambda a: pl.BlockSpec(a.shape, lambda b, i: (0,) * a.ndim)
    return pl.pallas_call(
        _ret_kernel,
        grid=(bsz, nt),
        in_specs=[pl.BlockSpec((None, c, w), lambda b, i: (b, i, 3)),
                  pl.BlockSpec((None, w, c), lambda b, i: (b, 1, i)),
                  pl.BlockSpec((None, c, w), lambda b, i: (b, i, 4)),
                  pl.BlockSpec((None, c, w), lambda b, i: (b, cb(i), 3)),
                  pl.BlockSpec((None, w, c), lambda b, i: (b, 1, cb(i))),
                  pl.BlockSpec((None, c, w), lambda b, i: (b, cb(i), 4)),
                  const(dmask), const(wend), const(cross), const(decs), const(bdm), const(gmat)],
        out_specs=[pl.BlockSpec((None, c, w), lambda b, i: (b, i, 0)),
                   pl.BlockSpec((None, c, w), lambda b, i: (b, cb(i), 0))],
        out_shape=[jax.ShapeDtypeStruct((bsz, s, w), BF16),
                   jax.ShapeDtypeStruct((bsz, s, w), BF16)],
        scratch_shapes=[pltpu.VMEM((2, w, w), F32)],
        compiler_params=_cparams(("parallel", "arbitrary")),
        name="retention",
    )(pn, pt, pn, pn, pt, pn, dmask, wend, cross, decs, bdm, gmat)


def _retention_tables(c):
    lg = jnp.asarray([math.log(1.0 - 2.0 ** (-5 - h)) for h in range(B_HEADS)], F32)
    j = jnp.arange(c, dtype=F32)
    dist = j[:, None] - j[None, :]
    df = jnp.where(dist >= 0, jnp.exp(lg[:, None, None] * jnp.maximum(dist, 0.0)), 0.0)
    dmask = jnp.stack([df, jnp.swapaxes(df, 1, 2)])
    row_head = (jnp.arange(MXU_TILE) % LANES) // 32
    col_head = jnp.arange(MXU_TILE) // 64
    wend_f = jnp.exp(lg[row_head][:, None] * (c - 1.0 - j)[None, :])
    wend_b = jnp.exp(lg[row_head][:, None] * j[None, :])
    cross_f = jnp.exp(lg[col_head][None, :] * (j + 1.0)[:, None])
    cross_b = jnp.exp(lg[col_head][None, :] * (c - j)[:, None])
    bdm = (row_head[:, None] == col_head[None, :]).astype(F32)
    decs = bdm * jnp.exp(lg * c)[col_head][None, :]
    gmat = ((col_head[:, None] == col_head[None, :]).astype(F32) / 64.0).astype(BF16)
    return (dmask, jnp.stack([wend_f, wend_b]), jnp.stack([cross_f, cross_b]), decs, bdm, gmat)


def _cd_kernel(nc, nt, seg_lens, glu_ref, glu_p, glu_n, pool_ref, pool_p, pool_n, wpool_ref,
               spool_ref, cw_ref, cb_ref, lng_ref, lnb_ref, wpw_ref, yc_ref, yd_ref, ext_ref):
    i = pl.program_id(1)
    r = pool_ref.shape[0]
    w = pool_ref.shape[1]
    prev_ok = ((i != 0) & (i != nc)).astype(F32)
    next_ok = ((i != nc - 1) & (i != nt - 1)).astype(F32)

    x = pool_ref[...].astype(F32)
    ext_ref[0:HALO, :] = pool_p[...].astype(F32) * prev_ok
    ext_ref[HALO:HALO + r, :] = x
    ext_ref[HALO + r:, :] = pool_n[...].astype(F32) * next_ok

    def sh(k):
        return ext_ref[HALO + k:HALO + k + r, :]

    w2 = sh(-1) + x
    w4 = w2 + sh(-2) + sh(1)
    w8 = w4 + sh(-4) + sh(-3) + sh(2) + sh(3)
    w16 = w8 + sh(-8) + sh(-7) + sh(-6) + sh(-5) + sh(4) + sh(5) + sh(6) + sh(7)
    grp = lax.broadcasted_iota(jnp.int32, (r, w), 1) >> 6
    wsum = jnp.where(grp == 0, w2, jnp.where(grp == 1, w4, jnp.where(grp == 2, w8, w16)))
    half = jnp.left_shift(1, grp)
    in_ctx = i < nc
    seg_len = jnp.where(in_ctx, seg_lens[0], seg_lens[1])
    pos = lax.broadcasted_iota(jnp.int32, (r, w), 0) + (i - jnp.where(in_ctx, 0, nc)) * r
    cnt = jnp.minimum(pos + half, seg_len) - jnp.maximum(pos - half, 0)
    dlt = wsum / cnt.astype(F32) - x
    yc_ref[...] = (_dot(dlt.astype(BF16), wpool_ref[...]) * spool_ref[...]).astype(yc_ref.dtype)

    def glu(ref):
        v = ref[...].astype(F32)
        return v[:, :w] * jax.nn.sigmoid(v[:, w:])

    ext_ref[0:HALO, :] = glu(glu_p) * prev_ok
    ext_ref[HALO:HALO + r, :] = glu(glu_ref)
    ext_ref[HALO + r:, :] = glu(glu_n) * next_ok
    sub = 64
    for r0 in range(0, r, sub):
        acc = jnp.zeros((sub, w), F32) + cb_ref[...]
        for k in range(CONV_K):
            o = HALO - CONV_K // 2 + k + r0
            acc = acc + ext_ref[o:o + sub, :] * cw_ref[k:k + 1, :]
        mu = jnp.mean(acc, axis=-1, keepdims=True)
        d = acc - mu
        var = jnp.mean(d * d, axis=-1, keepdims=True)
        hn = d * lax.rsqrt(var + 1e-5) * lng_ref[...] + lnb_ref[...]
        yd_ref[r0:r0 + sub, :] = _dot(jax.nn.silu(hn).astype(BF16), wpw_ref[...]).astype(yd_ref.dtype)


def _pool_conv(pn, wpool_bd, spool, conv_w, conv_b, ln_g, ln_b, w_pw, nc, seg_lens):
    bsz, s, _ = pn.shape
    r = ROW_TILE
    nt = s // r
    w = MXU_TILE
    hb = r // HALO
    nhb = s // HALO
    prev = lambda i: jnp.maximum(i * hb - 1, 0)
    nxt = lambda i: jnp.minimum((i + 1) * hb, nhb - 1)
    const = lambda a: pl.BlockSpec(a.shape, lambda b, i: (0,) * a.ndim)
    return pl.pallas_call(
        functools.partial(_cd_kernel, nc, nt, seg_lens),
        grid=(bsz, nt),
        in_specs=[pl.BlockSpec((None, r, 2 * w), lambda b, i: (b, i, 0)),
                  pl.BlockSpec((None, HALO, 2 * w), lambda b, i: (b, prev(i), 0)),
                  pl.BlockSpec((None, HALO, 2 * w), lambda b, i: (b, nxt(i), 0)),
                  pl.BlockSpec((None, r, w), lambda b, i: (b, i, 7)),
                  pl.BlockSpec((None, HALO, w), lambda b, i: (b, prev(i), 7)),
                  pl.BlockSpec((None, HALO, w), lambda b, i: (b, nxt(i), 7)),
                  const(wpool_bd), const(spool), const(conv_w), const(conv_b), const(ln_g),
                  const(ln_b), const(w_pw)],
        out_specs=[pl.BlockSpec((None, r, w), lambda b, i: (b, i, 0)),
                   pl.BlockSpec((None, r, w), lambda b, i: (b, i, 0))],
        out_shape=[jax.ShapeDtypeStruct((bsz, s, w), BF16),
                   jax.ShapeDtypeStruct((bsz, s, w), BF16)],
        scratch_shapes=[pltpu.VMEM((r + 2 * HALO, w), F32)],
        compiler_params=_cparams(("parallel", "parallel")),
        name="pool_conv",
    )(pn, pn, pn, pn, pn, pn, wpool_bd, spool, conv_w, conv_b, ln_g, ln_b, w_pw)


def _outproj_kernel(h_ref, mod_ref, g_ref, ya_ref, yf_ref, yb_ref, gf_ref, gb_ref, yc_ref, yd_ref,
                    w_ref, o_ref):
    yb = (jax.nn.silu(gf_ref[...].astype(F32)) * yf_ref[...].astype(F32)
          + jax.nn.silu(gb_ref[...].astype(F32)) * yb_ref[...].astype(F32))
    ycat = jnp.concatenate([ya_ref[...], yb.astype(BF16), yc_ref[...], yd_ref[...]], axis=-1)
    y = _dot(ycat, w_ref[...])
    o_ref[...] = h_ref[...] + mod_ref[2:3, :] * (_rms(y, 1e-6) * g_ref[...])


def _outproj(h, mod, g_post, ya, yf, yb, pn, yc, yd, w_out, nc):
    bsz, s, d = h.shape
    r = ROW_TILE
    nt = s // r
    w = MXU_TILE
    msel = lambda b, i: (jnp.where(i < nc, bsz, b), 0, 0)
    tile = lambda col: pl.BlockSpec((None, r, w), lambda b, i: (b, i, col))
    return pl.pallas_call(
        _outproj_kernel,
        grid=(bsz, nt),
        in_specs=[pl.BlockSpec((None, r, d), lambda b, i: (b, i, 0)),
                  pl.BlockSpec((None, 6, d), msel),
                  pl.BlockSpec((1, d), lambda b, i: (0, 0)),
                  tile(0), tile(0), tile(0), tile(5), tile(6), tile(0), tile(0),
                  pl.BlockSpec(w_out.shape, lambda b, i: (0, 0))],
        out_specs=pl.BlockSpec((None, r, d), lambda b, i: (b, i, 0)),
        out_shape=jax.ShapeDtypeStruct((bsz, s, d), F32),
        compiler_params=_cparams(("parallel", "parallel")),
        name="out_proj",
    )(h, mod, g_post, ya, yf, yb, pn, pn, yc, yd, w_out)


def _ffn_kernel(h_ref, mod_ref, gpre_ref, gpost_ref, wg_ref, wu_ref, wd_ref, o_ref):
    x = h_ref[...]
    u = _rms(x, 1e-6) * gpre_ref[...]
    ub = (u * (1.0 + mod_ref[4:5, :]) + mod_ref[3:4, :]).astype(BF16)
    a = (jax.nn.silu(_dot(ub, wg_ref[...])) * _dot(ub, wu_ref[...])).astype(BF16)
    f = _dot(a, wd_ref[...])
    o_ref[...] = x + mod_ref[5:6, :] * (_rms(f, 1e-6) * gpost_ref[...])


def _ffn(h, mod, g_pre, g_post, wg, wu, wd, nc):
    bsz, s, d = h.shape
    r = ROW_TILE
    nt = s // r
    msel = lambda b, i: (jnp.where(i < nc, bsz, b), 0, 0)
    const = lambda a: pl.BlockSpec(a.shape, lambda b, i: (0,) * a.ndim)
    return pl.pallas_call(
        _ffn_kernel,
        grid=(bsz, nt),
        in_specs=[pl.BlockSpec((None, r, d), lambda b, i: (b, i, 0)),
                  pl.BlockSpec((None, 6, d), msel),
                  const(g_pre), const(g_post), const(wg), const(wu), const(wd)],
        out_specs=pl.BlockSpec((None, r, d), lambda b, i: (b, i, 0)),
        out_shape=jax.ShapeDtypeStruct((bsz, s, d), F32),
        compiler_params=_cparams(("parallel", "parallel")),
        name="ffn",
    )(h, mod, g_pre, g_post, wg, wu, wd)


def _rope_tables(t_len, ctx_len, dqk_a, dk_b):
    rows = t_len // GRID_W
    row = jnp.repeat(jnp.arange(rows, dtype=F32), GRID_W)
    col = jnp.tile(jnp.arange(GRID_W, dtype=F32), rows)

    def cs(d, reps):
        n_freq = d // 4
        inv = ROPE_BASE ** (-jnp.arange(n_freq, dtype=F32) / n_freq)
        ang = jnp.concatenate([row[:, None] * inv, col[:, None] * inv], axis=-1)
        cos = jnp.concatenate([jnp.ones((ctx_len, d // 2), F32), jnp.cos(ang)], axis=0)
        sin = jnp.concatenate([jnp.zeros((ctx_len, d // 2), F32), jnp.sin(ang)], axis=0)
        return jnp.tile(cos, (1, reps)), jnp.tile(sin, (1, reps))

    ca, sa = cs(dqk_a, LANES // (dqk_a // 2))
    cb, sb = cs(dk_b, LANES // (dk_b // 2))
    tab_n = jnp.concatenate([ca, sa, cb, sb], axis=1)
    return tab_n, tab_n.T


def _rotary_perm(n_groups, d):
    half, g, i = jnp.meshgrid(jnp.arange(2), jnp.arange(n_groups), jnp.arange(d // 2), indexing="ij")
    return (g * d + half * (d // 2) + i).reshape(-1)


def kernel(x, c, ctx, c_ctx, w_ada, b_ada, g_pre_mix, g_post_mix, g_pre_ffn, g_post_ffn, w_in, w_out, lam_q1, lam_k1, lam_q2, lam_k2, g_subln, w_pool, s_pool, conv_w, conv_b, conv_ln_g, conv_ln_b, w_conv_out, w_ffn_gate, w_ffn_up, w_ffn_down):
    bsz, t_len, d = x.shape
    ctx_len = ctx.shape[1]
    depth = w_in.shape[0]
    assert d == 8 * A_HEADS * 32 and t_len % ROW_TILE == 0 and ctx_len % ROW_TILE == 0
    assert t_len % GRID_W == 0 and bsz < 8
    qw = d // 4
    dqk_a = qw // (2 * A_HEADS)
    dk_b = qw // B_HEADS
    nc = ctx_len // ROW_TILE

    h = jnp.concatenate([ctx, x], axis=1)

    cond = jnp.zeros((8, d), F32).at[:bsz].set(c).at[bsz].set(c_ctx)
    mods = _modulation(cond, w_ada, b_ada).reshape(depth, 8, 6, d)

    pa = _rotary_perm(2 * A_HEADS, dqk_a)
    pb = _rotary_perm(B_HEADS, dk_b)
    col = lambda k: w_in[:, :, k * qw:(k + 1) * qw]
    k_a, v_a, k_b, v_b, q_a, q_b, g_f, g_b, pool = (col(k) for k in range(9))
    glu = w_in[:, :, 9 * qw:]
    w_nat = jnp.concatenate([glu, k_a[:, :, pa], q_b[:, :, pb], v_b, g_f, g_b, pool], axis=-1).astype(BF16)
    w_tr = jnp.swapaxes(jnp.concatenate([q_a[:, :, pa], k_b[:, :, pb], v_a], axis=-1), 1, 2).astype(BF16)
    w_out_b = w_out.astype(BF16)
    wg_b, wu_b, wd_b = w_ffn_gate.astype(BF16), w_ffn_up.astype(BF16), w_ffn_down.astype(BF16)
    w_pw_b = w_conv_out.astype(BF16)
    eye = jnp.eye(len(POOL_WINDOWS), dtype=F32)
    wpool_bd = jnp.einsum("lgce,gh->lgche", w_pool, eye).reshape(depth, qw, qw).astype(BF16)
    conv_w2 = jnp.pad(conv_w.reshape(depth, CONV_K, qw), ((0, 0), (0, 1), (0, 0)))
    lamv = jnp.stack([lam_q1, lam_k1, lam_q2, lam_k2], axis=1)
    g_col = jnp.tile(g_subln, (1, 1))[:, :, None]

    tab_n, tab_t = _rope_tables(t_len, ctx_len, dqk_a, dk_b)
    ret_tabs = _retention_tables(ROW_TILE)
    row = lambda a, l: a[l][None, :]

    for l in range(depth):
        lam_init = 0.8 - 0.6 * math.exp(-0.3 * l)
        mod = mods[l]
        pn, pt = _inproj(h, mod, row(g_pre_mix, l), w_nat[l], w_tr[l], tab_n, tab_t, nc,
                         dqk_a ** -0.5, dk_b ** -0.5)
        ya = _attention(pn, pt, lamv[l], g_col[l], nc, lam_init)
        yf, yb = _retention(pn, pt, ret_tabs, nc)
        yc, yd = _pool_conv(pn, wpool_bd[l], row(s_pool, l), conv_w2[l], row(conv_b, l),
                            row(conv_ln_g, l), row(conv_ln_b, l), w_pw_b[l], nc, (ctx_len, t_len))
        h = _outproj(h, mod, row(g_post_mix, l), ya, yf, yb, pn, yc, yd, w_out_b[l], nc)
        h = _ffn(h, mod, row(g_pre_ffn, l), row(g_post_ffn, l), wg_b[l], wu_b[l], wd_b[l], nc)
    return h[:, ctx_len:, :]
```

```python
import functools
import math

import jax
import jax.numpy as jnp
from jax import lax
from jax.experimental import pallas as pl
from jax.experimental.pallas import tpu as pltpu

F32 = jnp.float32
BF16 = jnp.bfloat16

GRID_W = 64
ROPE_BASE = 10000.0
A_HEADS = 4
B_HEADS = 4
POOL_WINDOWS = (2, 4, 8, 16)
CONV_K = 31
LOG2_E = math.log2(math.e)

LANES = 128
MXU_TILE = 256
ROW_TILE = 256
HALO = 16
VMEM_LIMIT = 56 * 1024 * 1024
ATTN_KEY_TILE = 1024
ATTN_SKEW = 3
ATTN_SUM_ROWS = 16


def _cparams(sem):
    return pltpu.CompilerParams(dimension_semantics=sem, vmem_limit_bytes=VMEM_LIMIT)


def _rms(x, eps):
    return x * lax.rsqrt(jnp.mean(x * x, axis=-1, keepdims=True) + eps)


def _dot(a, b):
    return jnp.dot(a, b, preferred_element_type=F32)


def _mod_kernel(c_ref, w_ref, b_ref, o_ref):
    a = jax.nn.silu(c_ref[...])
    o_ref[...] = jnp.dot(a, w_ref[...], preferred_element_type=F32,
                         precision=lax.Precision.HIGHEST) + b_ref[...]


def _modulation(cond, w_ada, b_ada):
    n_layers, d, n6 = w_ada.shape
    tn = 2048
    return pl.pallas_call(
        _mod_kernel,
        grid=(n_layers, n6 // tn),
        in_specs=[pl.BlockSpec((8, d), lambda l, j: (0, 0)),
                  pl.BlockSpec((None, d, tn), lambda l, j: (l, 0, j)),
                  pl.BlockSpec((None, 1, tn), lambda l, j: (l, 0, j))],
        out_specs=pl.BlockSpec((None, 8, tn), lambda l, j: (l, 0, j)),
        out_shape=jax.ShapeDtypeStruct((n_layers, 8, n6), F32),
        compiler_params=_cparams(("parallel", "parallel")),
        name="adaln_mod",
    )(cond, w_ada, b_ada.reshape(n_layers, 1, n6))


def _inproj_kernel(a_scale, b_scale, h_ref, mod_ref, g_ref, wn_ref, wt_ref, tn_ref, tt_ref,
                   pn_ref, pt_ref):
    mod = mod_ref[...]
    u = _rms(h_ref[...], 1e-6) * g_ref[...]
    ub = (u * (1.0 + mod[1:2]) + mod[0:1]).astype(BF16)

    def nat(c0, c1):
        return _dot(ub, wn_ref[:, c0:c1])

    def tr(r0, r1):
        return lax.dot_general(wt_ref[r0:r1, :], ub, (((1,), (1,)), ((), ())),
                               preferred_element_type=F32)

    pn_ref[:, 0:512] = nat(0, 512).astype(BF16)
    ka = nat(512, 768)
    x1, x2 = ka[:, :LANES], ka[:, LANES:]
    cs, sn = tn_ref[:, 0:128], tn_ref[:, 128:256]
    pn_ref[:, 512:640] = (x1 * cs - x2 * sn).astype(BF16)
    pn_ref[:, 640:768] = (x1 * sn + x2 * cs).astype(BF16)
    qb = nat(768, 1024)
    x1, x2 = qb[:, :LANES], qb[:, LANES:]
    cs, sn = tn_ref[:, 256:384], tn_ref[:, 384:512]
    pn_ref[:, 768:896] = (x1 * cs - x2 * sn).astype(BF16)
    pn_ref[:, 896:1024] = (x1 * sn + x2 * cs).astype(BF16)
    pn_ref[:, 1024:2048] = nat(1024, 2048).astype(BF16)

    qa = tr(0, 256) * a_scale
    x1, x2 = qa[:LANES], qa[LANES:]
    cs, sn = tt_ref[0:128, :], tt_ref[128:256, :]
    pt_ref[0:128, :] = (x1 * cs - x2 * sn).astype(BF16)
    pt_ref[128:256, :] = (x1 * sn + x2 * cs).astype(BF16)
    kb = tr(256, 512) * b_scale
    x1, x2 = kb[:LANES], kb[LANES:]
    cs, sn = tt_ref[256:384, :], tt_ref[384:512, :]
    pt_ref[256:384, :] = (x1 * cs - x2 * sn).astype(BF16)
    pt_ref[384:512, :] = (x1 * sn + x2 * cs).astype(BF16)
    pt_ref[512:768, :] = tr(512, 768).astype(BF16)


def _inproj(h, mod, g_pre, w_nat, w_tr, tab_n, tab_t, nc, a_scale, b_scale):
    bsz, s, d = h.shape
    nt = s // ROW_TILE
    nn, ntr = w_nat.shape[1], w_tr.shape[0]
    msel = lambda b, i: (jnp.where(i < nc, bsz, b), 0, 0)
    return pl.pallas_call(
        functools.partial(_inproj_kernel, a_scale, b_scale),
        grid=(bsz, nt),
        in_specs=[pl.BlockSpec((None, ROW_TILE, d), lambda b, i: (b, i, 0)),
                  pl.BlockSpec((None, 6, d), msel),
                  pl.BlockSpec((1, d), lambda b, i: (0, 0)),
                  pl.BlockSpec((d, nn), lambda b, i: (0, 0)),
                  pl.BlockSpec((ntr, d), lambda b, i: (0, 0)),
                  pl.BlockSpec((ROW_TILE, 4 * LANES), lambda b, i: (i, 0)),
                  pl.BlockSpec((4 * LANES, ROW_TILE), lambda b, i: (0, i))],
        out_specs=[pl.BlockSpec((None, ROW_TILE, nn), lambda b, i: (b, i, 0)),
                   pl.BlockSpec((None, ntr, ROW_TILE), lambda b, i: (b, 0, i))],
        out_shape=[jax.ShapeDtypeStruct((bsz, s, nn), BF16),
                   jax.ShapeDtypeStruct((bsz, ntr, s), BF16)],
        compiler_params=_cparams(("parallel", "parallel")),
        name="in_proj",
    )(h, mod, g_pre, w_nat, w_tr, tab_n, tab_t)


def _attn_kernel(nc, n_ksteps, lam_init, qt_ref, kc_ref, vtc_ref, kl_ref, vtl_ref, lamv_ref, g_ref,
                 o_ref, qz_ref, m_ref, acc_ref):
    qi = pl.program_id(1)
    ki = pl.program_id(2)
    n_groups = 2 * A_HEADS
    dv = vtc_ref.shape[0] // A_HEADS

    def attend(k_ref, vt_ref):
        k = k_ref[...]
        ones = jnp.ones((ATTN_SUM_ROWS, k.shape[0]), BF16)
        s_q = [_dot(k, qz_ref[g]) for g in range(ATTN_SKEW)]
        for g in range(n_groups):
            hd = g // 2
            s = s_q.pop(0)
            if g + ATTN_SKEW < n_groups:
                s_q.append(_dot(k, qz_ref[g + ATTN_SKEW]))
            m_old = m_ref[g]
            m_new = jnp.maximum(m_old, jnp.max(s, axis=0, keepdims=True))
            p = jnp.exp2(s - m_new).astype(BF16)
            v_ext = jnp.concatenate([vt_ref[hd * dv:(hd + 1) * dv, :], ones], axis=0)
            acc_ref[g] = jnp.exp2(m_old - m_new) * acc_ref[g] + _dot(v_ext, p)
            m_ref[g] = m_new

    @pl.when(ki == 0)
    def _first():
        qt = qt_ref[...]
        rg = (lax.broadcasted_iota(jnp.int32, qt.shape, 0) & (LANES - 1)) >> 4
        for g in range(n_groups):
            qz_ref[g] = jnp.where(rg == g, qt, jnp.zeros_like(qt))
        m_ref[...] = jnp.full(m_ref.shape, -1e30, F32)
        acc_ref[...] = jnp.zeros(acc_ref.shape, F32)
        attend(kc_ref, vtc_ref)

    @pl.when((ki > 0) & (qi >= nc))
    def _latent():
        attend(kl_ref, vtl_ref)

    @pl.when(ki == n_ksteps - 1)
    def _fin():
        lv = lamv_ref[...]
        a1 = jnp.sum(lv[0:1] * lv[1:2], axis=-1, keepdims=True)
        a2 = jnp.sum(lv[2:3] * lv[3:4], axis=-1, keepdims=True)
        lam = jnp.exp(a1) - jnp.exp(a2) + lam_init
        outs = []
        for hd in range(A_HEADS):
            a_pos, a_neg = acc_ref[2 * hd], acc_ref[2 * hd + 1]
            o = a_pos[:dv] / a_pos[dv:dv + 1] - lam * (a_neg[:dv] / a_neg[dv:dv + 1])
            ms = jnp.mean(o * o, axis=0, keepdims=True)
            outs.append(o * lax.rsqrt(ms + 1e-5) * g_ref[...] * (1.0 - lam_init))
        o_ref[...] = jnp.concatenate(outs, axis=0).T.astype(o_ref.dtype)


def _attention(pn, pt, k_lat, vt_lat, lamv, g_col, nc, lam_init):
    bsz, s, _ = pn.shape
    t_len = k_lat.shape[1]
    ctx_len = s - t_len
    nq = s // ROW_TILE
    tk = min(ATTN_KEY_TILE, t_len)
    assert t_len % tk == 0
    n_ksteps = 1 + t_len // tk
    w = MXU_TILE
    dv_ext = w // A_HEADS + ATTN_SUM_ROWS
    kl = lambda qi, ki: jnp.where(qi < nc, 0, jnp.maximum(ki - 1, 0))
    return pl.pallas_call(
        functools.partial(_attn_kernel, nc, n_ksteps, lam_init),
        grid=(bsz, nq, n_ksteps),
        in_specs=[pl.BlockSpec((None, w, ROW_TILE), lambda b, qi, ki: (b, 0, qi)),
                  pl.BlockSpec((None, ctx_len, w), lambda b, qi, ki: (b, 0, 2)),
                  pl.BlockSpec((None, w, ctx_len), lambda b, qi, ki: (b, 2, 0)),
                  pl.BlockSpec((None, tk, w), lambda b, qi, ki: (b, kl(qi, ki), 0)),
                  pl.BlockSpec((None, w, tk), lambda b, qi, ki: (b, 0, kl(qi, ki))),
                  pl.BlockSpec(lamv.shape, lambda b, qi, ki: (0, 0)),
                  pl.BlockSpec(g_col.shape, lambda b, qi, ki: (0, 0))],
        out_specs=pl.BlockSpec((None, ROW_TILE, w), lambda b, qi, ki: (b, qi, 0)),
        out_shape=jax.ShapeDtypeStruct((bsz, s, w), BF16),
        scratch_shapes=[pltpu.VMEM((2 * A_HEADS, w, ROW_TILE), BF16),
                        pltpu.VMEM((2 * A_HEADS, 1, ROW_TILE), F32),
                        pltpu.VMEM((2 * A_HEADS, dv_ext, ROW_TILE), F32)],
        compiler_params=_cparams(("parallel", "parallel", "arbitrary")),
        name="diff_attn",
    )(pt, pn, pt, k_lat, vt_lat, lamv, g_col)


def _group_norm(y, gmat, eps):
    def gmean(z):
        zh = z.astype(BF16)
        zl = (z - zh.astype(F32)).astype(BF16)
        return _dot(zh, gmat) + _dot(zl, gmat)

    d = y - gmean(y)
    return d * lax.rsqrt(gmean(d * d) + eps)


def _ret_kernel(qf_ref, ktf_ref, vf_ref, qb_ref, ktb_ref, vb_ref, dm_ref, wend_ref, cross_ref,
                decs_ref, bd_ref, gm_ref, yf_ref, yb_ref, s_ref):
    i = pl.program_id(1)

    @pl.when(i == 0)
    def _init():
        s_ref[...] = jnp.zeros(s_ref.shape, F32)

    def one_dir(dr, q_ref, kt_ref, v_ref, y_ref):
        q = q_ref[...]
        kt = kt_ref[...]
        v = v_ref[...]
        rh = (lax.broadcasted_iota(jnp.int32, kt.shape, 0) & (LANES - 1)) >> 5
        ch = lax.broadcasted_iota(jnp.int32, (q.shape[0], v.shape[1]), 1) >> 6
        y = jnp.zeros((q.shape[0], v.shape[1]), F32)
        for hd in range(B_HEADS):
            ktz = jnp.where(rh == hd, kt, jnp.zeros_like(kt))
            p = (_dot(q, ktz) * dm_ref[dr, hd]).astype(BF16)
            y = jnp.where(ch == hd, _dot(p, v), y)
        s_old = s_ref[dr]
        y = y + _dot(q, s_old.astype(BF16)) * cross_ref[dr]
        kw = (kt.astype(F32) * wend_ref[dr]).astype(BF16)
        s_ref[dr] = decs_ref[...] * s_old + bd_ref[...] * _dot(kw, v)
        y_ref[...] = _group_norm(y, gm_ref[...], 1e-6).astype(y_ref.dtype)

    one_dir(0, qf_ref, ktf_ref, vf_ref, yf_ref)
    one_dir(1, qb_ref, ktb_ref, vb_ref, yb_ref)


def _retention(pn, pt, tabs, nc):
    bsz, s, _ = pn.shape
    c = ROW_TILE
    nt = s // c
    w = MXU_TILE
    dmask, wend, cross, decs, bdm, gmat = tabs
    cb = lambda i: jnp.where(i < nc, nc - 1 - i, nt - 1 - (i - nc))
    const = lambda a: pl.BlockSpec(a.shape, lambda b, i: (0,) * a.ndim)
    return pl.pallas_call(
        _ret_kernel,
        grid=(bsz, nt),
        in_specs=[pl.BlockSpec((None, c, w), lambda b, i: (b, i, 3)),
                  pl.BlockSpec((None, w, c), lambda b, i: (b, 1, i)),
                  pl.BlockSpec((None, c, w), lambda b, i: (b, i, 4)),
                  pl.BlockSpec((None, c, w), lambda b, i: (b, cb(i), 3)),
                  pl.BlockSpec((None, w, c), lambda b, i: (b, 1, cb(i))),
                  pl.BlockSpec((None, c, w), lambda b, i: (b, cb(i), 4)),
                  const(dmask), const(wend), const(cross), const(decs), const(bdm), const(gmat)],
        out_specs=[pl.BlockSpec((None, c, w), lambda b, i: (b, i, 0)),
                   pl.BlockSpec((None, c, w), lambda b, i: (b, cb(i), 0))],
        out_shape=[jax.ShapeDtypeStruct((bsz, s, w), BF16),
                   jax.ShapeDtypeStruct((bsz, s, w), BF16)],
        scratch_shapes=[pltpu.VMEM((2, w, w), F32)],
        compiler_params=_cparams(("parallel", "arbitrary")),
        name="retention",
    )(pn, pt, pn, pn, pt, pn, dmask, wend, cross, decs, bdm, gmat)


def _retention_tables(c):
    lg = jnp.asarray([math.log(1.0 - 2.0 ** (-5 - h)) for h in range(B_HEADS)], F32)
    j = jnp.arange(c, dtype=F32)
    dist = j[:, None] - j[None, :]
    df = jnp.where(dist >= 0, jnp.exp(lg[:, None, None] * jnp.maximum(dist, 0.0)), 0.0)
    dmask = jnp.stack([df, jnp.swapaxes(df, 1, 2)])
    row_head = (jnp.arange(MXU_TILE) % LANES) // 32
    col_head = jnp.arange(MXU_TILE) // 64
    wend_f = jnp.exp(lg[row_head][:, None] * (c - 1.0 - j)[None, :])
    wend_b = jnp.exp(lg[row_head][:, None] * j[None, :])
    cross_f = jnp.exp(lg[col_head][None, :] * (j + 1.0)[:, None])
    cross_b = jnp.exp(lg[col_head][None, :] * (c - j)[:, None])
    bdm = (row_head[:, None] == col_head[None, :]).astype(F32)
    decs = bdm * jnp.exp(lg * c)[col_head][None, :]
    gmat = ((col_head[:, None] == col_head[None, :]).astype(F32) / 64.0).astype(BF16)
    return (dmask, jnp.stack([wend_f, wend_b]), jnp.stack([cross_f, cross_b]), decs, bdm, gmat)


def _cd_kernel(nc, nt, seg_lens, glu_ref, glu_p, glu_n, pool_ref, pool_p, pool_n, wpool_ref,
               spool_ref, cw_ref, cb_ref, lng_ref, lnb_ref, wpw_ref, yc_ref, yd_ref, ext_ref):
    i = pl.program_id(1)
    r = pool_ref.shape[0]
    w = pool_ref.shape[1]
    prev_ok = ((i != 0) & (i != nc)).astype(F32)
    next_ok = ((i != nc - 1) & (i != nt - 1)).astype(F32)

    x = pool_ref[...].astype(F32)
    ext_ref[0:HALO, :] = pool_p[...].astype(F32) * prev_ok
    ext_ref[HALO:HALO + r, :] = x
    ext_ref[HALO + r:, :] = pool_n[...].astype(F32) * next_ok

    def sh(k):
        return ext_ref[HALO + k:HALO + k + r, :]

    w2 = sh(-1) + x
    w4 = w2 + sh(-2) + sh(1)
    w8 = w4 + sh(-4) + sh(-3) + sh(2) + sh(3)
    w16 = w8 + sh(-8) + sh(-7) + sh(-6) + sh(-5) + sh(4) + sh(5) + sh(6) + sh(7)
    grp = lax.broadcasted_iota(jnp.int32, (r, w), 1) >> 6
    wsum = jnp.where(grp == 0, w2, jnp.where(grp == 1, w4, jnp.where(grp == 2, w8, w16)))
    half = jnp.left_shift(1, grp)
    in_ctx = i < nc
    seg_len = jnp.where(in_ctx, seg_lens[0], seg_lens[1])
    pos = lax.broadcasted_iota(jnp.int32, (r, w), 0) + (i - jnp.where(in_ctx, 0, nc)) * r
    cnt = jnp.minimum(pos + half, seg_len) - jnp.maximum(pos - half, 0)
    dlt = wsum / cnt.astype(F32) - x
    yc_ref[...] = (_dot(dlt.astype(BF16), wpool_ref[...]) * spool_ref[...]).astype(yc_ref.dtype)

    def glu(ref):
        v = ref[...].astype(F32)
        return v[:, :w] * jax.nn.sigmoid(v[:, w:])

    ext_ref[0:HALO, :] = glu(glu_p) * prev_ok
    ext_ref[HALO:HALO + r, :] = glu(glu_ref)
    ext_ref[HALO + r:, :] = glu(glu_n) * next_ok
    sub = 64
    for r0 in range(0, r, sub):
        acc = jnp.zeros((sub, w), F32) + cb_ref[...]
        for k in range(CONV_K):
            o = HALO - CONV_K // 2 + k + r0
            acc = acc + ext_ref[o:o + sub, :] * cw_ref[k:k + 1, :]
        mu = jnp.mean(acc, axis=-1, keepdims=True)
        d = acc - mu
        var = jnp.mean(d * d, axis=-1, keepdims=True)
        hn = d * lax.rsqrt(var + 1e-5) * lng_ref[...] + lnb_ref[...]
        yd_ref[r0:r0 + sub, :] = _dot(jax.nn.silu(hn).astype(BF16), wpw_ref[...]).astype(yd_ref.dtype)


def _pool_conv(pn, wpool_bd, spool, conv_w, conv_b, ln_g, ln_b, w_pw, nc, seg_lens):
    bsz, s, _ = pn.shape
    r = ROW_TILE
    nt = s // r
    w = MXU_TILE
    hb = r // HALO
    nhb = s // HALO
    prev = lambda i: jnp.maximum(i * hb - 1, 0)
    nxt = lambda i: jnp.minimum((i + 1) * hb, nhb - 1)
    const = lambda a: pl.BlockSpec(a.shape, lambda b, i: (0,) * a.ndim)
    return pl.pallas_call(
        functools.partial(_cd_kernel, nc, nt, seg_lens),
        grid=(bsz, nt),
        in_specs=[pl.BlockSpec((None, r, 2 * w), lambda b, i: (b, i, 0)),
                  pl.BlockSpec((None, HALO, 2 * w), lambda b, i: (b, prev(i), 0)),
                  pl.BlockSpec((None, HALO, 2 * w), lambda b, i: (b, nxt(i), 0)),
                  pl.BlockSpec((None, r, w), lambda b, i: (b, i, 7)),
                  pl.BlockSpec((None, HALO, w), lambda b, i: (b, prev(i), 7)),
                  pl.BlockSpec((None, HALO, w), lambda b, i: (b, nxt(i), 7)),
                  const(wpool_bd), const(spool), const(conv_w), const(conv_b), const(ln_g),
                  const(ln_b), const(w_pw)],
        out_specs=[pl.BlockSpec((None, r, w), lambda b, i: (b, i, 0)),
                   pl.BlockSpec((None, r, w), lambda b, i: (b, i, 0))],
        out_shape=[jax.ShapeDtypeStruct((bsz, s, w), BF16),
                   jax.ShapeDtypeStruct((bsz, s, w), BF16)],
        scratch_shapes=[pltpu.VMEM((r + 2 * HALO, w), F32)],
        compiler_params=_cparams(("parallel", "parallel")),
        name="pool_conv",
    )(pn, pn, pn, pn, pn, pn, wpool_bd, spool, conv_w, conv_b, ln_g, ln_b, w_pw)


def _outproj_kernel(h_ref, mod_ref, g_ref, ya_ref, yf_ref, yb_ref, gf_ref, gb_ref, yc_ref, yd_ref,
                    w_ref, o_ref):
    yb = (jax.nn.silu(gf_ref[...].astype(F32)) * yf_ref[...].astype(F32)
          + jax.nn.silu(gb_ref[...].astype(F32)) * yb_ref[...].astype(F32))
    ycat = jnp.concatenate([ya_ref[...], yb.astype(BF16), yc_ref[...], yd_ref[...]], axis=-1)
    y = _dot(ycat, w_ref[...])
    o_ref[...] = h_ref[...] + mod_ref[2:3, :] * (_rms(y, 1e-6) * g_ref[...])


def _outproj(h, mod, g_post, ya, yf, yb, pn, yc, yd, w_out, nc):
    bsz, s, d = h.shape
    r = ROW_TILE
    nt = s // r
    w = MXU_TILE
    msel = lambda b, i: (jnp.where(i < nc, bsz, b), 0, 0)
    tile = lambda col: pl.BlockSpec((None, r, w), lambda b, i: (b, i, col))
    return pl.pallas_call(
        _outproj_kernel,
        grid=(bsz, nt),
        in_specs=[pl.BlockSpec((None, r, d), lambda b, i: (b, i, 0)),
                  pl.BlockSpec((None, 6, d), msel),
                  pl.BlockSpec((1, d), lambda b, i: (0, 0)),
                  tile(0), tile(0), tile(0), tile(5), tile(6), tile(0), tile(0),
                  pl.BlockSpec(w_out.shape, lambda b, i: (0, 0))],
        out_specs=pl.BlockSpec((None, r, d), lambda b, i: (b, i, 0)),
        out_shape=jax.ShapeDtypeStruct((bsz, s, d), F32),
        compiler_params=_cparams(("parallel", "parallel")),
        name="out_proj",
    )(h, mod, g_post, ya, yf, yb, pn, pn, yc, yd, w_out)


def _ffn_kernel(h_ref, mod_ref, gpre_ref, gpost_ref, wg_ref, wu_ref, wd_ref, o_ref):
    x = h_ref[...]
    u = _rms(x, 1e-6) * gpre_ref[...]
    ub = (u * (1.0 + mod_ref[4:5, :]) + mod_ref[3:4, :]).astype(BF16)
    a = (jax.nn.silu(_dot(ub, wg_ref[...])) * _dot(ub, wu_ref[...])).astype(BF16)
    f = _dot(a, wd_ref[...])
    o_ref[...] = x + mod_ref[5:6, :] * (_rms(f, 1e-6) * gpost_ref[...])


def _ffn(h, mod, g_pre, g_post, wg, wu, wd, nc):
    bsz, s, d = h.shape
    r = ROW_TILE
    nt = s // r
    msel = lambda b, i: (jnp.where(i < nc, bsz, b), 0, 0)
    const = lambda a: pl.BlockSpec(a.shape, lambda b, i: (0,) * a.ndim)
    return pl.pallas_call(
        _ffn_kernel,
        grid=(bsz, nt),
        in_specs=[pl.BlockSpec((None, r, d), lambda b, i: (b, i, 0)),
                  pl.BlockSpec((None, 6, d), msel),
                  const(g_pre), const(g_post), const(wg), const(wu), const(wd)],
        out_specs=pl.BlockSpec((None, r, d), lambda b, i: (b, i, 0)),
        out_shape=jax.ShapeDtypeStruct((bsz, s, d), F32),
        compiler_params=_cparams(("parallel", "parallel")),
        name="ffn",
    )(h, mod, g_pre, g_post, wg, wu, wd)


def _rope_tables(t_len, ctx_len, dqk_a, dk_b):
    rows = t_len // GRID_W
    row = jnp.repeat(jnp.arange(rows, dtype=F32), GRID_W)
    col = jnp.tile(jnp.arange(GRID_W, dtype=F32), rows)

    def cs(d, reps):
        n_freq = d // 4
        inv = ROPE_BASE ** (-jnp.arange(n_freq, dtype=F32) / n_freq)
        ang = jnp.concatenate([row[:, None] * inv, col[:, None] * inv], axis=-1)
        cos = jnp.concatenate([jnp.ones((ctx_len, d // 2), F32), jnp.cos(ang)], axis=0)
        sin = jnp.concatenate([jnp.zeros((ctx_len, d // 2), F32), jnp.sin(ang)], axis=0)
        return jnp.tile(cos, (1, reps)), jnp.tile(sin, (1, reps))

    ca, sa = cs(dqk_a, LANES // (dqk_a // 2))
    cb, sb = cs(dk_b, LANES // (dk_b // 2))
    tab_n = jnp.concatenate([ca, sa, cb, sb], axis=1)
    return tab_n, tab_n.T


def _rotary_perm(n_groups, d):
    half, g, i = jnp.meshgrid(jnp.arange(2), jnp.arange(n_groups), jnp.arange(d // 2), indexing="ij")
    return (g * d + half * (d // 2) + i).reshape(-1)


def kernel(x, c, ctx, c_ctx, w_ada, b_ada, g_pre_mix, g_post_mix, g_pre_ffn, g_post_ffn, w_in, w_out, lam_q1, lam_k1, lam_q2, lam_k2, g_subln, w_pool, s_pool, conv_w, conv_b, conv_ln_g, conv_ln_b, w_conv_out, w_ffn_gate, w_ffn_up, w_ffn_down):
    bsz, t_len, d = x.shape
    ctx_len = ctx.shape[1]
    depth = w_in.shape[0]
    assert d == 8 * A_HEADS * 32 and t_len % ROW_TILE == 0 and ctx_len % ROW_TILE == 0
    assert t_len % GRID_W == 0 and bsz < 8
    qw = d // 4
    dqk_a = qw // (2 * A_HEADS)
    dk_b = qw // B_HEADS
    nc = ctx_len // ROW_TILE

    h = jnp.concatenate([ctx, x], axis=1)

    cond = jnp.zeros((8, d), F32).at[:bsz].set(c).at[bsz].set(c_ctx)
    mods = _modulation(cond, w_ada, b_ada).reshape(depth, 8, 6, d)

    pa = _rotary_perm(2 * A_HEADS, dqk_a)
    pb = _rotary_perm(B_HEADS, dk_b)
    col = lambda k: w_in[:, :, k * qw:(k + 1) * qw]
    k_a, v_a, k_b, v_b, q_a, q_b, g_f, g_b, pool = (col(k) for k in range(9))
    glu = w_in[:, :, 9 * qw:]
    w_nat = jnp.concatenate([glu, k_a[:, :, pa], q_b[:, :, pb], v_b, g_f, g_b, pool], axis=-1).astype(BF16)
    w_tr = jnp.swapaxes(jnp.concatenate([q_a[:, :, pa], k_b[:, :, pb], v_a], axis=-1), 1, 2).astype(BF16)
    w_out_b = w_out.astype(BF16)
    wg_b, wu_b, wd_b = w_ffn_gate.astype(BF16), w_ffn_up.astype(BF16), w_ffn_down.astype(BF16)
    w_pw_b = w_conv_out.astype(BF16)
    eye = jnp.eye(len(POOL_WINDOWS), dtype=F32)
    wpool_bd = jnp.einsum("lgce,gh->lgche", w_pool, eye).reshape(depth, qw, qw).astype(BF16)
    conv_w2 = jnp.pad(conv_w.reshape(depth, CONV_K, qw), ((0, 0), (0, 1), (0, 0)))
    lamv = jnp.stack([lam_q1, lam_k1, lam_q2, lam_k2], axis=1)
    g_col = jnp.tile(g_subln, (1, 1))[:, :, None]

    tab_n, tab_t = _rope_tables(t_len, ctx_len, dqk_a, dk_b)
    ret_tabs = _retention_tables(ROW_TILE)
    row = lambda a, l: a[l][None, :]

    for l in range(depth):
        lam_init = 0.8 - 0.6 * math.exp(-0.3 * l)
        mod = mods[l]
        pn, pt = _inproj(h, mod, row(g_pre_mix, l), w_nat[l], w_tr[l], tab_n, tab_t, nc,
                         dqk_a ** -0.5 * LOG2_E, dk_b ** -0.5)
        k_lat = pn[:, ctx_len:, 2 * qw:3 * qw]
        vt_lat = pt[:, 2 * qw:3 * qw, ctx_len:]
        ya = _attention(pn, pt, k_lat, vt_lat, lamv[l], g_col[l], nc, lam_init)
        yf, yb = _retention(pn, pt, ret_tabs, nc)
        yc, yd = _pool_conv(pn, wpool_bd[l], row(s_pool, l), conv_w2[l], row(conv_b, l),
                            row(conv_ln_g, l), row(conv_ln_b, l), w_pw_b[l], nc, (ctx_len, t_len))
        h = _outproj(h, mod, row(g_post_mix, l), ya, yf, yb, pn, yc, yd, w_out_b[l], nc)
        h = _ffn(h, mod, row(g_pre_ffn, l), row(g_post_ffn, l), wg_b[l], wu_b[l], wd_b[l], nc)
    return h[:, ctx_len:, :]
```

```python
import functools
import math

import jax
import jax.numpy as jnp
from jax import lax
from jax.experimental import pallas as pl
from jax.experimental.pallas import tpu as pltpu

F32 = jnp.float32
BF16 = jnp.bfloat16

GRID_W = 64
ROPE_BASE = 10000.0
A_HEADS = 4
B_HEADS = 4
POOL_WINDOWS = (2, 4, 8, 16)
CONV_K = 31
LOG2_E = math.log2(math.e)

LANES = 128
MXU_TILE = 256
ROW_TILE = 256
HALO = 16
VMEM_LIMIT = 56 * 1024 * 1024
ATTN_KEY_TILE = 1024
ATTN_SKEW = 3
ATTN_HEADROOM = 8.0
ATTN_SUM_ROWS = 16


def _cparams(sem):
    return pltpu.CompilerParams(dimension_semantics=sem, vmem_limit_bytes=VMEM_LIMIT)


def _rms(x, eps):
    return x * lax.rsqrt(jnp.mean(x * x, axis=-1, keepdims=True) + eps)


def _dot(a, b):
    return jnp.dot(a, b, preferred_element_type=F32)


def _mod_kernel(c_ref, w_ref, b_ref, o_ref):
    a = jax.nn.silu(c_ref[...])
    o_ref[...] = jnp.dot(a, w_ref[...], preferred_element_type=F32,
                         precision=lax.Precision.HIGHEST) + b_ref[...]


def _modulation(cond, w_ada, b_ada):
    n_layers, d, n6 = w_ada.shape
    tn = 2048
    return pl.pallas_call(
        _mod_kernel,
        grid=(n_layers, n6 // tn),
        in_specs=[pl.BlockSpec((8, d), lambda l, j: (0, 0)),
                  pl.BlockSpec((None, d, tn), lambda l, j: (l, 0, j)),
                  pl.BlockSpec((None, 1, tn), lambda l, j: (l, 0, j))],
        out_specs=pl.BlockSpec((None, 8, tn), lambda l, j: (l, 0, j)),
        out_shape=jax.ShapeDtypeStruct((n_layers, 8, n6), F32),
        compiler_params=_cparams(("parallel", "parallel")),
        name="adaln_mod",
    )(cond, w_ada, b_ada.reshape(n_layers, 1, n6))


def _inproj_kernel(a_scale, b_scale, h_ref, mod_ref, g_ref, wn_ref, wt_ref, tn_ref, tt_ref,
                   pn_ref, pt_ref):
    mod = mod_ref[...]
    u = _rms(h_ref[...], 1e-6) * g_ref[...]
    ub = (u * (1.0 + mod[1:2]) + mod[0:1]).astype(BF16)

    def nat(c0, c1):
        return _dot(ub, wn_ref[:, c0:c1])

    def tr(r0, r1):
        return lax.dot_general(wt_ref[r0:r1, :], ub, (((1,), (1,)), ((), ())),
                               preferred_element_type=F32)

    pn_ref[:, 0:512] = nat(0, 512).astype(BF16)
    ka = nat(512, 768)
    x1, x2 = ka[:, :LANES], ka[:, LANES:]
    cs, sn = tn_ref[:, 0:128], tn_ref[:, 128:256]
    pn_ref[:, 512:640] = (x1 * cs - x2 * sn).astype(BF16)
    pn_ref[:, 640:768] = (x1 * sn + x2 * cs).astype(BF16)
    qb = nat(768, 1024)
    x1, x2 = qb[:, :LANES], qb[:, LANES:]
    cs, sn = tn_ref[:, 256:384], tn_ref[:, 384:512]
    pn_ref[:, 768:896] = (x1 * cs - x2 * sn).astype(BF16)
    pn_ref[:, 896:1024] = (x1 * sn + x2 * cs).astype(BF16)
    pn_ref[:, 1024:2048] = nat(1024, 2048).astype(BF16)

    qa = tr(0, 256) * a_scale
    x1, x2 = qa[:LANES], qa[LANES:]
    cs, sn = tt_ref[0:128, :], tt_ref[128:256, :]
    pt_ref[0:128, :] = (x1 * cs - x2 * sn).astype(BF16)
    pt_ref[128:256, :] = (x1 * sn + x2 * cs).astype(BF16)
    kb = tr(256, 512) * b_scale
    x1, x2 = kb[:LANES], kb[LANES:]
    cs, sn = tt_ref[256:384, :], tt_ref[384:512, :]
    pt_ref[256:384, :] = (x1 * cs - x2 * sn).astype(BF16)
    pt_ref[384:512, :] = (x1 * sn + x2 * cs).astype(BF16)
    pt_ref[512:768, :] = tr(512, 768).astype(BF16)


def _inproj(h, mod, g_pre, w_nat, w_tr, tab_n, tab_t, nc, a_scale, b_scale):
    bsz, s, d = h.shape
    nt = s // ROW_TILE
    nn, ntr = w_nat.shape[1], w_tr.shape[0]
    msel = lambda b, i: (jnp.where(i < nc, bsz, b), 0, 0)
    return pl.pallas_call(
        functools.partial(_inproj_kernel, a_scale, b_scale),
        grid=(bsz, nt),
        in_specs=[pl.BlockSpec((None, ROW_TILE, d), lambda b, i: (b, i, 0)),
                  pl.BlockSpec((None, 6, d), msel),
                  pl.BlockSpec((1, d), lambda b, i: (0, 0)),
                  pl.BlockSpec((d, nn), lambda b, i: (0, 0)),
                  pl.BlockSpec((ntr, d), lambda b, i: (0, 0)),
                  pl.BlockSpec((ROW_TILE, 4 * LANES), lambda b, i: (i, 0)),
                  pl.BlockSpec((4 * LANES, ROW_TILE), lambda b, i: (0, i))],
        out_specs=[pl.BlockSpec((None, ROW_TILE, nn), lambda b, i: (b, i, 0)),
                   pl.BlockSpec((None, ntr, ROW_TILE), lambda b, i: (b, 0, i))],
        out_shape=[jax.ShapeDtypeStruct((bsz, s, nn), BF16),
                   jax.ShapeDtypeStruct((bsz, ntr, s), BF16)],
        compiler_params=_cparams(("parallel", "parallel")),
        name="in_proj",
    )(h, mod, g_pre, w_nat, w_tr, tab_n, tab_t)


def _attn_kernel(nc, n_ksteps, lam_init, qt_ref, kc_ref, vtc_ref, kl_ref, vtl_ref, lamv_ref, g_ref,
                 o_ref, qz_ref, m_ref, acc_ref, tmax_ref, pv_ref):
    qi = pl.program_id(1)
    ki = pl.program_id(2)
    n_groups = 2 * A_HEADS
    dv = vtc_ref.shape[0] // A_HEADS

    def attend(k_ref, vt_ref):
        k = k_ref[...]
        ones = jnp.ones((ATTN_SUM_ROWS, k.shape[0]), BF16)
        s_q = [_dot(k, qz_ref[g]) for g in range(ATTN_SKEW)]
        for g in range(n_groups):
            hd = g // 2
            s = s_q.pop(0)
            if g + ATTN_SKEW < n_groups:
                s_q.append(_dot(k, qz_ref[g + ATTN_SKEW]))
            m_old = m_ref[g]
            m_new = jnp.maximum(m_old, jnp.max(s, axis=0, keepdims=True))
            p = jnp.exp2((s - m_new).astype(BF16))
            v_ext = jnp.concatenate([vt_ref[hd * dv:(hd + 1) * dv, :], ones], axis=0)
            acc_ref[g] = jnp.exp2(m_old - m_new) * acc_ref[g] + _dot(v_ext, p)
            m_ref[g] = m_new

    def attend_lagged(k_ref, vt_ref):
        k = k_ref[...]
        ones = jnp.ones((ATTN_SUM_ROWS, k.shape[0]), BF16)
        s_q = [_dot(k, qz_ref[g]) for g in range(ATTN_SKEW)]
        excess = None
        for g in range(n_groups):
            hd = g // 2
            s = s_q.pop(0)
            if g + ATTN_SKEW < n_groups:
                s_q.append(_dot(k, qz_ref[g + ATTN_SKEW]))
            m_fix = m_ref[g]
            p = jnp.exp2((s - m_fix).astype(BF16))
            t_max = jnp.max(s, axis=0, keepdims=True)
            tmax_ref[g] = t_max
            excess = t_max - m_fix if excess is None else jnp.maximum(excess, t_max - m_fix)
            v_ext = jnp.concatenate([vt_ref[hd * dv:(hd + 1) * dv, :], ones], axis=0)
            pv_ref[g] = _dot(v_ext, p)
        return jnp.max(excess)

    def commit_lagged():
        for g in range(n_groups):
            m_old = m_ref[g]
            m_new = jnp.maximum(m_old, tmax_ref[g])
            acc_ref[g] = (acc_ref[g] + pv_ref[g]) * jnp.exp2(m_old - m_new)
            m_ref[g] = m_new

    @pl.when(ki == 0)
    def _first():
        qt = qt_ref[...]
        rg = (lax.broadcasted_iota(jnp.int32, qt.shape, 0) & (LANES - 1)) >> 4
        for g in range(n_groups):
            qz_ref[g] = jnp.where(rg == g, qt, jnp.zeros_like(qt))
        m_ref[...] = jnp.full(m_ref.shape, -1e30, F32)
        acc_ref[...] = jnp.zeros(acc_ref.shape, F32)
        attend(kc_ref, vtc_ref)

    @pl.when((ki > 0) & (qi >= nc))
    def _latent():
        within = attend_lagged(kl_ref, vtl_ref) <= ATTN_HEADROOM

        @pl.when(within)
        def _commit():
            commit_lagged()

        @pl.when(jnp.logical_not(within))
        def _redo():
            attend(kl_ref, vtl_ref)

    @pl.when(ki == n_ksteps - 1)
    def _fin():
        lv = lamv_ref[...]
        a1 = jnp.sum(lv[0:1] * lv[1:2], axis=-1, keepdims=True)
        a2 = jnp.sum(lv[2:3] * lv[3:4], axis=-1, keepdims=True)
        lam = jnp.exp(a1) - jnp.exp(a2) + lam_init
        outs = []
        for hd in range(A_HEADS):
            a_pos, a_neg = acc_ref[2 * hd], acc_ref[2 * hd + 1]
            o = a_pos[:dv] / a_pos[dv:dv + 1] - lam * (a_neg[:dv] / a_neg[dv:dv + 1])
            ms = jnp.mean(o * o, axis=0, keepdims=True)
            outs.append(o * lax.rsqrt(ms + 1e-5) * g_ref[...] * (1.0 - lam_init))
        o_ref[...] = jnp.concatenate(outs, axis=0).T.astype(o_ref.dtype)


def _attention(pn, pt, k_lat, vt_lat, lamv, g_col, nc, lam_init):
    bsz, s, _ = pn.shape
    t_len = k_lat.shape[1]
    ctx_len = s - t_len
    nq = s // ROW_TILE
    tk = min(ATTN_KEY_TILE, t_len)
    assert t_len % tk == 0
    n_ksteps = 1 + t_len // tk
    w = MXU_TILE
    dv_ext = w // A_HEADS + ATTN_SUM_ROWS
    kl = lambda qi, ki: jnp.where(qi < nc, 0, jnp.maximum(ki - 1, 0))
    return pl.pallas_call(
        functools.partial(_attn_kernel, nc, n_ksteps, lam_init),
        grid=(bsz, nq, n_ksteps),
        in_specs=[pl.BlockSpec((None, w, ROW_TILE), lambda b, qi, ki: (b, 0, qi)),
                  pl.BlockSpec((None, ctx_len, w), lambda b, qi, ki: (b, 0, 2)),
                  pl.BlockSpec((None, w, ctx_len), lambda b, qi, ki: (b, 2, 0)),
                  pl.BlockSpec((None, tk, w), lambda b, qi, ki: (b, kl(qi, ki), 0)),
                  pl.BlockSpec((None, w, tk), lambda b, qi, ki: (b, 0, kl(qi, ki))),
                  pl.BlockSpec(lamv.shape, lambda b, qi, ki: (0, 0)),
                  pl.BlockSpec(g_col.shape, lambda b, qi, ki: (0, 0))],
        out_specs=pl.BlockSpec((None, ROW_TILE, w), lambda b, qi, ki: (b, qi, 0)),
        out_shape=jax.ShapeDtypeStruct((bsz, s, w), BF16),
        scratch_shapes=[pltpu.VMEM((2 * A_HEADS, w, ROW_TILE), BF16),
                        pltpu.VMEM((2 * A_HEADS, 1, ROW_TILE), F32),
                        pltpu.VMEM((2 * A_HEADS, dv_ext, ROW_TILE), F32),
                        pltpu.VMEM((2 * A_HEADS, 1, ROW_TILE), F32),
                        pltpu.VMEM((2 * A_HEADS, dv_ext, ROW_TILE), F32)],
        compiler_params=_cparams(("parallel", "parallel", "arbitrary")),
        name="diff_attn",
    )(pt, pn, pt, k_lat, vt_lat, lamv, g_col)


def _group_norm(y, gmat, eps):
    def gmean(z):
        zh = z.astype(BF16)
        zl = (z - zh.astype(F32)).astype(BF16)
        return _dot(zh, gmat) + _dot(zl, gmat)

    d = y - gmean(y)
    return d * lax.rsqrt(gmean(d * d) + eps)


def _ret_kernel(qf_ref, ktf_ref, vf_ref, qb_ref, ktb_ref, vb_ref, dm_ref, wend_ref, cross_ref,
                decs_ref, bd_ref, gm_ref, yf_ref, yb_ref, s_ref):
    i = pl.program_id(1)

    @pl.when(i == 0)
    def _init():
        s_ref[...] = jnp.zeros(s_ref.shape, F32)

    def one_dir(dr, q_ref, kt_ref, v_ref, y_ref):
        q = q_ref[...]
        kt = kt_ref[...]
        v = v_ref[...]
        rh = (lax.broadcasted_iota(jnp.int32, kt.shape, 0) & (LANES - 1)) >> 5
        ch = lax.broadcasted_iota(jnp.int32, (q.shape[0], v.shape[1]), 1) >> 6
        y = jnp.zeros((q.shape[0], v.shape[1]), F32)
        for hd in range(B_HEADS):
            ktz = jnp.where(rh == hd, kt, jnp.zeros_like(kt))
            p = (_dot(q, ktz) * dm_ref[dr, hd]).astype(BF16)
            y = jnp.where(ch == hd, _dot(p, v), y)
        s_old = s_ref[dr]
        y = y + _dot(q, s_old.astype(BF16)) * cross_ref[dr]
        kw = (kt.astype(F32) * wend_ref[dr]).astype(BF16)
        s_ref[dr] = decs_ref[...] * s_old + bd_ref[...] * _dot(kw, v)
        y_ref[...] = _group_norm(y, gm_ref[...], 1e-6).astype(y_ref.dtype)

    one_dir(0, qf_ref, ktf_ref, vf_ref, yf_ref)
    one_dir(1, qb_ref, ktb_ref, vb_ref, yb_ref)


def _retention(pn, pt, tabs, nc):
    bsz, s, _ = pn.shape
    c = ROW_TILE
    nt = s // c
    w = MXU_TILE
    dmask, wend, cross, decs, bdm, gmat = tabs
    cb = lambda i: jnp.where(i < nc, nc - 1 - i, nt - 1 - (i - nc))
    const = lambda a: pl.BlockSpec(a.shape, lambda b, i: (0,) * a.ndim)
    return pl.pallas_call(
        _ret_kernel,
        grid=(bsz, nt),
        in_specs=[pl.BlockSpec((None, c, w), lambda b, i: (b, i, 3)),
                  pl.BlockSpec((None, w, c), lambda b, i: (b, 1, i)),
                  pl.BlockSpec((None, c, w), lambda b, i: (b, i, 4)),
                  pl.BlockSpec((None, c, w), lambda b, i: (b, cb(i), 3)),
                  pl.BlockSpec((None, w, c), lambda b, i: (b, 1, cb(i))),
                  pl.BlockSpec((None, c, w), lambda b, i: (b, cb(i), 4)),
                  const(dmask), const(wend), const(cross), const(decs), const(bdm), const(gmat)],
        out_specs=[pl.BlockSpec((None, c, w), lambda b, i: (b, i, 0)),
                   pl.BlockSpec((None, c, w), lambda b, i: (b, cb(i), 0))],
        out_shape=[jax.ShapeDtypeStruct((bsz, s, w), BF16),
                   jax.ShapeDtypeStruct((bsz, s, w), BF16)],
        scratch_shapes=[pltpu.VMEM((2, w, w), F32)],
        compiler_params=_cparams(("parallel", "arbitrary")),
        name="retention",
    )(pn, pt, pn, pn, pt, pn, dmask, wend, cross, decs, bdm, gmat)


def _retention_tables(c):
    lg = jnp.asarray([math.log(1.0 - 2.0 ** (-5 - h)) for h in range(B_HEADS)], F32)
    j = jnp.arange(c, dtype=F32)
    dist = j[:, None] - j[None, :]
    df = jnp.where(dist >= 0, jnp.exp(lg[:, None, None] * jnp.maximum(dist, 0.0)), 0.0)
    dmask = jnp.stack([df, jnp.swapaxes(df, 1, 2)])
    row_head = (jnp.arange(MXU_TILE) % LANES) // 32
    col_head = jnp.arange(MXU_TILE) // 64
    wend_f = jnp.exp(lg[row_head][:, None] * (c - 1.0 - j)[None, :])
    wend_b = jnp.exp(lg[row_head][:, None] * j[None, :])
    cross_f = jnp.exp(lg[col_head][None, :] * (j + 1.0)[:, None])
    cross_b = jnp.exp(lg[col_head][None, :] * (c - j)[:, None])
    bdm = (row_head[:, None] == col_head[None, :]).astype(F32)
    decs = bdm * jnp.exp(lg * c)[col_head][None, :]
    gmat = ((col_head[:, None] == col_head[None, :]).astype(F32) / 64.0).astype(BF16)
    return (dmask, jnp.stack([wend_f, wend_b]), jnp.stack([cross_f, cross_b]), decs, bdm, gmat)


def _cd_kernel(nc, nt, seg_lens, glu_ref, glu_p, glu_n, pool_ref, pool_p, pool_n, wpool_ref,
               spool_ref, cw_ref, cb_ref, lng_ref, lnb_ref, wpw_ref, yc_ref, yd_ref, ext_ref):
    i = pl.program_id(1)
    r = pool_ref.shape[0]
    w = pool_ref.shape[1]
    prev_ok = ((i != 0) & (i != nc)).astype(F32)
    next_ok = ((i != nc - 1) & (i != nt - 1)).astype(F32)

    x = pool_ref[...].astype(F32)
    ext_ref[0:HALO, :] = pool_p[...].astype(F32) * prev_ok
    ext_ref[HALO:HALO + r, :] = x
    ext_ref[HALO + r:, :] = pool_n[...].astype(F32) * next_ok

    def sh(k):
        return ext_ref[HALO + k:HALO + k + r, :]

    w2 = sh(-1) + x
    w4 = w2 + sh(-2) + sh(1)
    w8 = w4 + sh(-4) + sh(-3) + sh(2) + sh(3)
    w16 = w8 + sh(-8) + sh(-7) + sh(-6) + sh(-5) + sh(4) + sh(5) + sh(6) + sh(7)
    grp = lax.broadcasted_iota(jnp.int32, (r, w), 1) >> 6
    wsum = jnp.where(grp == 0, w2, jnp.where(grp == 1, w4, jnp.where(grp == 2, w8, w16)))
    half = jnp.left_shift(1, grp)
    in_ctx = i < nc
    seg_len = jnp.where(in_ctx, seg_lens[0], seg_lens[1])
    pos = lax.broadcasted_iota(jnp.int32, (r, w), 0) + (i - jnp.where(in_ctx, 0, nc)) * r
    cnt = jnp.minimum(pos + half, seg_len) - jnp.maximum(pos - half, 0)
    dlt = wsum / cnt.astype(F32) - x
    yc_ref[...] = (_dot(dlt.astype(BF16), wpool_ref[...]) * spool_ref[...]).astype(yc_ref.dtype)

    def glu(ref):
        v = ref[...].astype(F32)
        return v[:, :w] * jax.nn.sigmoid(v[:, w:])

    ext_ref[0:HALO, :] = glu(glu_p) * prev_ok
    ext_ref[HALO:HALO + r, :] = glu(glu_ref)
    ext_ref[HALO + r:, :] = glu(glu_n) * next_ok
    sub = 64
    for r0 in range(0, r, sub):
        acc = jnp.zeros((sub, w), F32) + cb_ref[...]
        for k in range(CONV_K):
            o = HALO - CONV_K // 2 + k + r0
            acc = acc + ext_ref[o:o + sub, :] * cw_ref[k:k + 1, :]
        mu = jnp.mean(acc, axis=-1, keepdims=True)
        d = acc - mu
        var = jnp.mean(d * d, axis=-1, keepdims=True)
        hn = d * lax.rsqrt(var + 1e-5) * lng_ref[...] + lnb_ref[...]
        yd_ref[r0:r0 + sub, :] = _dot(jax.nn.silu(hn).astype(BF16), wpw_ref[...]).astype(yd_ref.dtype)


def _pool_conv(pn, wpool_bd, spool, conv_w, conv_b, ln_g, ln_b, w_pw, nc, seg_lens):
    bsz, s, _ = pn.shape
    r = ROW_TILE
    nt = s // r
    w = MXU_TILE
    hb = r // HALO
    nhb = s // HALO
    prev = lambda i: jnp.maximum(i * hb - 1, 0)
    nxt = lambda i: jnp.minimum((i + 1) * hb, nhb - 1)
    const = lambda a: pl.BlockSpec(a.shape, lambda b, i: (0,) * a.ndim)
    return pl.pallas_call(
        functools.partial(_cd_kernel, nc, nt, seg_lens),
        grid=(bsz, nt),
        in_specs=[pl.BlockSpec((None, r, 2 * w), lambda b, i: (b, i, 0)),
                  pl.BlockSpec((None, HALO, 2 * w), lambda b, i: (b, prev(i), 0)),
                  pl.BlockSpec((None, HALO, 2 * w), lambda b, i: (b, nxt(i), 0)),
                  pl.BlockSpec((None, r, w), lambda b, i: (b, i, 7)),
                  pl.BlockSpec((None, HALO, w), lambda b, i: (b, prev(i), 7)),
                  pl.BlockSpec((None, HALO, w), lambda b, i: (b, nxt(i), 7)),
                  const(wpool_bd), const(spool), const(conv_w), const(conv_b), const(ln_g),
                  const(ln_b), const(w_pw)],
        out_specs=[pl.BlockSpec((None, r, w), lambda b, i: (b, i, 0)),
                   pl.BlockSpec((None, r, w), lambda b, i: (b, i, 0))],
        out_shape=[jax.ShapeDtypeStruct((bsz, s, w), BF16),
                   jax.ShapeDtypeStruct((bsz, s, w), BF16)],
        scratch_shapes=[pltpu.VMEM((r + 2 * HALO, w), F32)],
        compiler_params=_cparams(("parallel", "parallel")),
        name="pool_conv",
    )(pn, pn, pn, pn, pn, pn, wpool_bd, spool, conv_w, conv_b, ln_g, ln_b, w_pw)


def _outproj_kernel(h_ref, mod_ref, g_ref, ya_ref, yf_ref, yb_ref, gf_ref, gb_ref, yc_ref, yd_ref,
                    w_ref, o_ref):
    yb = (jax.nn.silu(gf_ref[...].astype(F32)) * yf_ref[...].astype(F32)
          + jax.nn.silu(gb_ref[...].astype(F32)) * yb_ref[...].astype(F32))
    ycat = jnp.concatenate([ya_ref[...], yb.astype(BF16), yc_ref[...], yd_ref[...]], axis=-1)
    y = _dot(ycat, w_ref[...])
    o_ref[...] = h_ref[...] + mod_ref[2:3, :] * (_rms(y, 1e-6) * g_ref[...])


def _outproj(h, mod, g_post, ya, yf, yb, pn, yc, yd, w_out, nc):
    bsz, s, d = h.shape
    r = ROW_TILE
    nt = s // r
    w = MXU_TILE
    msel = lambda b, i: (jnp.where(i < nc, bsz, b), 0, 0)
    tile = lambda col: pl.BlockSpec((None, r, w), lambda b, i: (b, i, col))
    return pl.pallas_call(
        _outproj_kernel,
        grid=(bsz, nt),
        in_specs=[pl.BlockSpec((None, r, d), lambda b, i: (b, i, 0)),
                  pl.BlockSpec((None, 6, d), msel),
                  pl.BlockSpec((1, d), lambda b, i: (0, 0)),
                  tile(0), tile(0), tile(0), tile(5), tile(6), tile(0), tile(0),
                  pl.BlockSpec(w_out.shape, lambda b, i: (0, 0))],
        out_specs=pl.BlockSpec((None, r, d), lambda b, i: (b, i, 0)),
        out_shape=jax.ShapeDtypeStruct((bsz, s, d), F32),
        compiler_params=_cparams(("parallel", "parallel")),
        name="out_proj",
    )(h, mod, g_post, ya, yf, yb, pn, pn, yc, yd, w_out)


def _ffn_kernel(h_ref, mod_ref, gpre_ref, gpost_ref, wg_ref, wu_ref, wd_ref, o_ref):
    x = h_ref[...]
    u = _rms(x, 1e-6) * gpre_ref[...]
    ub = (u * (1.0 + mod_ref[4:5, :]) + mod_ref[3:4, :]).astype(BF16)
    a = (jax.nn.silu(_dot(ub, wg_ref[...])) * _dot(ub, wu_ref[...])).astype(BF16)
    f = _dot(a, wd_ref[...])
    o_ref[...] = x + mod_ref[5:6, :] * (_rms(f, 1e-6) * gpost_ref[...])


def _ffn(h, mod, g_pre, g_post, wg, wu, wd, nc):
    bsz, s, d = h.shape
    r = ROW_TILE
    nt = s // r
    msel = lambda b, i: (jnp.where(i < nc, bsz, b), 0, 0)
    const = lambda a: pl.BlockSpec(a.shape, lambda b, i: (0,) * a.ndim)
    return pl.pallas_call(
        _ffn_kernel,
        grid=(bsz, nt),
        in_specs=[pl.BlockSpec((None, r, d), lambda b, i: (b, i, 0)),
                  pl.BlockSpec((None, 6, d), msel),
                  const(g_pre), const(g_post), const(wg), const(wu), const(wd)],
        out_specs=pl.BlockSpec((None, r, d), lambda b, i: (b, i, 0)),
        out_shape=jax.ShapeDtypeStruct((bsz, s, d), F32),
        compiler_params=_cparams(("parallel", "parallel")),
        name="ffn",
    )(h, mod, g_pre, g_post, wg, wu, wd)


def _rope_tables(t_len, ctx_len, dqk_a, dk_b):
    rows = t_len // GRID_W
    row = jnp.repeat(jnp.arange(rows, dtype=F32), GRID_W)
    col = jnp.tile(jnp.arange(GRID_W, dtype=F32), rows)

    def cs(d, reps):
        n_freq = d // 4
        inv = ROPE_BASE ** (-jnp.arange(n_freq, dtype=F32) / n_freq)
        ang = jnp.concatenate([row[:, None] * inv, col[:, None] * inv], axis=-1)
        cos = jnp.concatenate([jnp.ones((ctx_len, d // 2), F32), jnp.cos(ang)], axis=0)
        sin = jnp.concatenate([jnp.zeros((ctx_len, d // 2), F32), jnp.sin(ang)], axis=0)
        return jnp.tile(cos, (1, reps)), jnp.tile(sin, (1, reps))

    ca, sa = cs(dqk_a, LANES // (dqk_a // 2))
    cb, sb = cs(dk_b, LANES // (dk_b // 2))
    tab_n = jnp.concatenate([ca, sa, cb, sb], axis=1)
    return tab_n, tab_n.T


def _rotary_perm(n_groups, d):
    half, g, i = jnp.meshgrid(jnp.arange(2), jnp.arange(n_groups), jnp.arange(d // 2), indexing="ij")
    return (g * d + half * (d // 2) + i).reshape(-1)


def kernel(x, c, ctx, c_ctx, w_ada, b_ada, g_pre_mix, g_post_mix, g_pre_ffn, g_post_ffn, w_in, w_out, lam_q1, lam_k1, lam_q2, lam_k2, g_subln, w_pool, s_pool, conv_w, conv_b, conv_ln_g, conv_ln_b, w_conv_out, w_ffn_gate, w_ffn_up, w_ffn_down):
    bsz, t_len, d = x.shape
    ctx_len = ctx.shape[1]
    depth = w_in.shape[0]
    assert d == 8 * A_HEADS * 32 and t_len % ROW_TILE == 0 and ctx_len % ROW_TILE == 0
    assert t_len % GRID_W == 0 and bsz < 8
    qw = d // 4
    dqk_a = qw // (2 * A_HEADS)
    dk_b = qw // B_HEADS
    nc = ctx_len // ROW_TILE

    h = jnp.concatenate([ctx, x], axis=1)

    cond = jnp.zeros((8, d), F32).at[:bsz].set(c).at[bsz].set(c_ctx)
    mods = _modulation(cond, w_ada, b_ada).reshape(depth, 8, 6, d)

    pa = _rotary_perm(2 * A_HEADS, dqk_a)
    pb = _rotary_perm(B_HEADS, dk_b)
    col = lambda k: w_in[:, :, k * qw:(k + 1) * qw]
    k_a, v_a, k_b, v_b, q_a, q_b, g_f, g_b, pool = (col(k) for k in range(9))
    glu = w_in[:, :, 9 * qw:]
    w_nat = jnp.concatenate([glu, k_a[:, :, pa], q_b[:, :, pb], v_b, g_f, g_b, pool], axis=-1).astype(BF16)
    w_tr = jnp.swapaxes(jnp.concatenate([q_a[:, :, pa], k_b[:, :, pb], v_a], axis=-1), 1, 2).astype(BF16)
    w_out_b = w_out.astype(BF16)
    wg_b, wu_b, wd_b = w_ffn_gate.astype(BF16), w_ffn_up.astype(BF16), w_ffn_down.astype(BF16)
    w_pw_b = w_conv_out.astype(BF16)
    eye = jnp.eye(len(POOL_WINDOWS), dtype=F32)
    wpool_bd = jnp.einsum("lgce,gh->lgche", w_pool, eye).reshape(depth, qw, qw).astype(BF16)
    conv_w2 = jnp.pad(conv_w.reshape(depth, CONV_K, qw), ((0, 0), (0, 1), (0, 0)))
    lamv = jnp.stack([lam_q1, lam_k1, lam_q2, lam_k2], axis=1)
    g_col = jnp.tile(g_subln, (1, 1))[:, :, None]

    tab_n, tab_t = _rope_tables(t_len, ctx_len, dqk_a, dk_b)
    ret_tabs = _retention_tables(ROW_TILE)
    row = lambda a, l: a[l][None, :]

    for l in range(depth):
        lam_init = 0.8 - 0.6 * math.exp(-0.3 * l)
        mod = mods[l]
        pn, pt = _inproj(h, mod, row(g_pre_mix, l), w_nat[l], w_tr[l], tab_n, tab_t, nc,
                         dqk_a ** -0.5 * LOG2_E, dk_b ** -0.5)
        k_lat = pn[:, ctx_len:, 2 * qw:3 * qw]
        vt_lat = pt[:, 2 * qw:3 * qw, ctx_len:]
        ya = _attention(pn, pt, k_lat, vt_lat, lamv[l], g_col[l], nc, lam_init)
        yf, yb = _retention(pn, pt, ret_tabs, nc)
        yc, yd = _pool_conv(pn, wpool_bd[l], row(s_pool, l), conv_w2[l], row(conv_b, l),
                            row(conv_ln_g, l), row(conv_ln_b, l), w_pw_b[l], nc, (ctx_len, t_len))
        h = _outproj(h, mod, row(g_post_mix, l), ya, yf, yb, pn, yc, yd, w_out_b[l], nc)
        h = _ffn(h, mod, row(g_pre_ffn, l), row(g_post_ffn, l), wg_b[l], wu_b[l], wd_b[l], nc)
    return h[:, ctx_len:, :]
```

```python
import functools
import math

import jax
import jax.numpy as jnp
from jax import lax
from jax.experimental import pallas as pl
from jax.experimental.pallas import tpu as pltpu

F32 = jnp.float32
BF16 = jnp.bfloat16

GRID_W = 64
ROPE_BASE = 10000.0
A_HEADS = 4
B_HEADS = 4
POOL_WINDOWS = (2, 4, 8, 16)
CONV_K = 31
LOG2_E = math.log2(math.e)

LANES = 128
SUBLANES = 8
MXU_TILE = 256
ROW_TILE = 256
HALO = 16
VMEM_LIMIT = 56 * 1024 * 1024
ATTN_KEY_TILE = 1024
ATTN_SKEW = 3
ATTN_HEADROOM = 8.0
ATTN_SUM_ROWS = 16


def _cparams(sem):
    return pltpu.CompilerParams(dimension_semantics=sem, vmem_limit_bytes=VMEM_LIMIT)


def _rms(x, eps):
    return x * lax.rsqrt(jnp.mean(x * x, axis=-1, keepdims=True) + eps)


def _dot(a, b):
    return jnp.dot(a, b, preferred_element_type=F32)


def _mod_kernel(c_ref, w_ref, b_ref, o_ref):
    a = jax.nn.silu(c_ref[...])
    o_ref[...] = jnp.dot(a, w_ref[...], preferred_element_type=F32,
                         precision=lax.Precision.HIGHEST) + b_ref[...]


def _modulation(cond, w_ada, b_ada):
    n_layers, d, n6 = w_ada.shape
    tn = 2048
    return pl.pallas_call(
        _mod_kernel,
        grid=(n_layers, n6 // tn),
        in_specs=[pl.BlockSpec((8, d), lambda l, j: (0, 0)),
                  pl.BlockSpec((None, d, tn), lambda l, j: (l, 0, j)),
                  pl.BlockSpec((None, 1, tn), lambda l, j: (l, 0, j))],
        out_specs=pl.BlockSpec((None, 8, tn), lambda l, j: (l, 0, j)),
        out_shape=jax.ShapeDtypeStruct((n_layers, 8, n6), F32),
        compiler_params=_cparams(("parallel", "parallel")),
        name="adaln_mod",
    )(cond, w_ada, b_ada.reshape(n_layers, 1, n6))


def _inproj_kernel(a_scale, b_scale, h_ref, mod_ref, g_ref, wn_ref, wt_ref, tn_ref, tt_ref,
                   pn_ref, pt_ref):
    mod = mod_ref[...]
    u = _rms(h_ref[...], 1e-6) * g_ref[...]
    ub = (u * (1.0 + mod[1:2]) + mod[0:1]).astype(BF16)

    def nat(c0, c1):
        return _dot(ub, wn_ref[:, c0:c1])

    def tr(r0, r1):
        return lax.dot_general(wt_ref[r0:r1, :], ub, (((1,), (1,)), ((), ())),
                               preferred_element_type=F32)

    pn_ref[:, 0:512] = nat(0, 512).astype(BF16)
    ka = nat(512, 768)
    x1, x2 = ka[:, :LANES], ka[:, LANES:]
    cs, sn = tn_ref[:, 0:128], tn_ref[:, 128:256]
    pn_ref[:, 512:640] = (x1 * cs - x2 * sn).astype(BF16)
    pn_ref[:, 640:768] = (x1 * sn + x2 * cs).astype(BF16)
    qb = nat(768, 1024)
    x1, x2 = qb[:, :LANES], qb[:, LANES:]
    cs, sn = tn_ref[:, 256:384], tn_ref[:, 384:512]
    pn_ref[:, 768:896] = (x1 * cs - x2 * sn).astype(BF16)
    pn_ref[:, 896:1024] = (x1 * sn + x2 * cs).astype(BF16)
    pn_ref[:, 1024:2048] = nat(1024, 2048).astype(BF16)

    qa = tr(0, 256) * a_scale
    x1, x2 = qa[:LANES], qa[LANES:]
    cs, sn = tt_ref[0:128, :], tt_ref[128:256, :]
    pt_ref[0:128, :] = (x1 * cs - x2 * sn).astype(BF16)
    pt_ref[128:256, :] = (x1 * sn + x2 * cs).astype(BF16)
    kb = tr(256, 512) * b_scale
    x1, x2 = kb[:LANES], kb[LANES:]
    cs, sn = tt_ref[256:384, :], tt_ref[384:512, :]
    pt_ref[256:384, :] = (x1 * cs - x2 * sn).astype(BF16)
    pt_ref[384:512, :] = (x1 * sn + x2 * cs).astype(BF16)
    pt_ref[512:768, :] = tr(512, 768).astype(BF16)


def _inproj(h, mod, g_pre, w_nat, w_tr, tab_n, tab_t, nc, a_scale, b_scale):
    bsz, s, d = h.shape
    nt = s // ROW_TILE
    nn, ntr = w_nat.shape[1], w_tr.shape[0]
    msel = lambda b, i: (jnp.where(i < nc, bsz, b), 0, 0)
    return pl.pallas_call(
        functools.partial(_inproj_kernel, a_scale, b_scale),
        grid=(bsz, nt),
        in_specs=[pl.BlockSpec((None, ROW_TILE, d), lambda b, i: (b, i, 0)),
                  pl.BlockSpec((None, 6, d), msel),
                  pl.BlockSpec((1, d), lambda b, i: (0, 0)),
                  pl.BlockSpec((d, nn), lambda b, i: (0, 0)),
                  pl.BlockSpec((ntr, d), lambda b, i: (0, 0)),
                  pl.BlockSpec((ROW_TILE, 4 * LANES), lambda b, i: (i, 0)),
                  pl.BlockSpec((4 * LANES, ROW_TILE), lambda b, i: (0, i))],
        out_specs=[pl.BlockSpec((None, ROW_TILE, nn), lambda b, i: (b, i, 0)),
                   pl.BlockSpec((None, ntr, ROW_TILE), lambda b, i: (b, 0, i))],
        out_shape=[jax.ShapeDtypeStruct((bsz, s, nn), BF16),
                   jax.ShapeDtypeStruct((bsz, ntr, s), BF16)],
        compiler_params=_cparams(("parallel", "parallel")),
        name="in_proj",
    )(h, mod, g_pre, w_nat, w_tr, tab_n, tab_t)


def _attn_kernel(nc, n_ksteps, lam_init, qt_ref, kc_ref, vtc_ref, kl_ref, vtl_ref, lamv_ref, g_ref,
                 o_ref, qz_ref, m_ref, acc_ref, tmax_ref, pv_ref):
    qi = pl.program_id(1)
    ki = pl.program_id(2)
    n_groups = 2 * A_HEADS
    dv = vtc_ref.shape[0] // A_HEADS

    def attend(k_ref, vt_ref):
        k = k_ref[...]
        ones = jnp.ones((ATTN_SUM_ROWS, k.shape[0]), BF16)
        s_q = [_dot(k, qz_ref[g]) for g in range(ATTN_SKEW)]
        for g in range(n_groups):
            hd = g // 2
            s = s_q.pop(0)
            if g + ATTN_SKEW < n_groups:
                s_q.append(_dot(k, qz_ref[g + ATTN_SKEW]))
            m_old = m_ref[g]
            m_new = jnp.maximum(m_old, jnp.max(s, axis=0, keepdims=True))
            p = jnp.exp2((s - m_new).astype(BF16))
            v_ext = jnp.concatenate([vt_ref[hd * dv:(hd + 1) * dv, :], ones], axis=0)
            acc_ref[g] = jnp.exp2(m_old - m_new) * acc_ref[g] + _dot(v_ext, p)
            m_ref[g] = m_new

    def attend_lagged(k_ref, vt_ref):
        k = k_ref[...]
        ones = jnp.ones((ATTN_SUM_ROWS, k.shape[0]), BF16)
        s_q = [_dot(k, qz_ref[g]) for g in range(ATTN_SKEW)]
        excess = None
        for g in range(n_groups):
            hd = g // 2
            s = s_q.pop(0)
            if g + ATTN_SKEW < n_groups:
                s_q.append(_dot(k, qz_ref[g + ATTN_SKEW]))
            m_fix = m_ref[g]
            p = jnp.exp2((s - m_fix).astype(BF16))
            t_max = jnp.max(s, axis=0, keepdims=True)
            tmax_ref[g] = t_max
            excess = t_max - m_fix if excess is None else jnp.maximum(excess, t_max - m_fix)
            v_ext = jnp.concatenate([vt_ref[hd * dv:(hd + 1) * dv, :], ones], axis=0)
            pv_ref[g] = _dot(v_ext, p)
        return jnp.max(excess)

    def commit_lagged():
        for g in range(n_groups):
            m_old = m_ref[g]
            m_new = jnp.maximum(m_old, tmax_ref[g])
            acc_ref[g] = (acc_ref[g] + pv_ref[g]) * jnp.exp2(m_old - m_new)
            m_ref[g] = m_new

    @pl.when(ki == 0)
    def _first():
        qt = qt_ref[...]
        rg = (lax.broadcasted_iota(jnp.int32, qt.shape, 0) & (LANES - 1)) >> 4
        for g in range(n_groups):
            qz_ref[g] = jnp.where(rg == g, qt, jnp.zeros_like(qt))
        m_ref[...] = jnp.full(m_ref.shape, -1e30, F32)
        acc_ref[...] = jnp.zeros(acc_ref.shape, F32)
        attend(kc_ref, vtc_ref)

    @pl.when((ki > 0) & (qi >= nc))
    def _latent():
        within = attend_lagged(kl_ref, vtl_ref) <= ATTN_HEADROOM

        @pl.when(within)
        def _commit():
            commit_lagged()

        @pl.when(jnp.logical_not(within))
        def _redo():
            attend(kl_ref, vtl_ref)

    @pl.when(ki == n_ksteps - 1)
    def _fin():
        lv = lamv_ref[...]
        a1 = jnp.sum(lv[0:1] * lv[1:2], axis=-1, keepdims=True)
        a2 = jnp.sum(lv[2:3] * lv[3:4], axis=-1, keepdims=True)
        lam = jnp.exp(a1) - jnp.exp(a2) + lam_init
        outs = []
        for hd in range(A_HEADS):
            a_pos, a_neg = acc_ref[2 * hd], acc_ref[2 * hd + 1]
            o = a_pos[:dv] / a_pos[dv:dv + 1] - lam * (a_neg[:dv] / a_neg[dv:dv + 1])
            ms = jnp.mean(o * o, axis=0, keepdims=True)
            outs.append(o * lax.rsqrt(ms + 1e-5) * g_ref[...] * (1.0 - lam_init))
        o_ref[...] = jnp.concatenate(outs, axis=0).T.astype(o_ref.dtype)


def _attention(pn, pt, k_lat, vt_lat, lamv, g_col, nc, lam_init):
    bsz, s, _ = pn.shape
    t_len = k_lat.shape[1]
    ctx_len = s - t_len
    nq = s // ROW_TILE
    tk = min(ATTN_KEY_TILE, t_len)
    assert t_len % tk == 0
    n_ksteps = 1 + t_len // tk
    w = MXU_TILE
    dv_ext = w // A_HEADS + ATTN_SUM_ROWS
    kl = lambda qi, ki: jnp.where(qi < nc, 0, jnp.maximum(ki - 1, 0))
    return pl.pallas_call(
        functools.partial(_attn_kernel, nc, n_ksteps, lam_init),
        grid=(bsz, nq, n_ksteps),
        in_specs=[pl.BlockSpec((None, w, ROW_TILE), lambda b, qi, ki: (b, 0, qi)),
                  pl.BlockSpec((None, ctx_len, w), lambda b, qi, ki: (b, 0, 2)),
                  pl.BlockSpec((None, w, ctx_len), lambda b, qi, ki: (b, 2, 0)),
                  pl.BlockSpec((None, tk, w), lambda b, qi, ki: (b, kl(qi, ki), 0)),
                  pl.BlockSpec((None, w, tk), lambda b, qi, ki: (b, 0, kl(qi, ki))),
                  pl.BlockSpec(lamv.shape, lambda b, qi, ki: (0, 0)),
                  pl.BlockSpec(g_col.shape, lambda b, qi, ki: (0, 0))],
        out_specs=pl.BlockSpec((None, ROW_TILE, w), lambda b, qi, ki: (b, qi, 0)),
        out_shape=jax.ShapeDtypeStruct((bsz, s, w), BF16),
        scratch_shapes=[pltpu.VMEM((2 * A_HEADS, w, ROW_TILE), BF16),
                        pltpu.VMEM((2 * A_HEADS, 1, ROW_TILE), F32),
                        pltpu.VMEM((2 * A_HEADS, dv_ext, ROW_TILE), F32),
                        pltpu.VMEM((2 * A_HEADS, 1, ROW_TILE), F32),
                        pltpu.VMEM((2 * A_HEADS, dv_ext, ROW_TILE), F32)],
        compiler_params=_cparams(("parallel", "parallel", "arbitrary")),
        name="diff_attn",
    )(pt, pn, pt, k_lat, vt_lat, lamv, g_col)


def _group_norm(y, gmat, eps):
    yh = y.astype(BF16)
    yl = (y - yh.astype(F32)).astype(BF16)
    d = y - (_dot(yh, gmat) + _dot(yl, gmat))
    return d * lax.rsqrt(_dot((d * d).astype(BF16), gmat) + eps)


def _ret_kernel(qf_ref, ktf_ref, vf_ref, qb_ref, ktb_ref, vb_ref, dm_ref, wend_ref, cross_ref,
                decs_ref, bd_ref, gm_ref, yf_ref, yb_ref, s_ref):
    i = pl.program_id(1)

    @pl.when(i == 0)
    def _init():
        s_ref[...] = jnp.zeros(s_ref.shape, F32)

    def one_dir(dr, q_ref, kt_ref, v_ref, y_ref):
        q = q_ref[...]
        kt = kt_ref[...]
        v = v_ref[...]
        rh = (lax.broadcasted_iota(jnp.int32, kt.shape, 0) & (LANES - 1)) >> 5
        ch = lax.broadcasted_iota(jnp.int32, (q.shape[0], v.shape[1]), 1) >> 6
        y = jnp.zeros((q.shape[0], v.shape[1]), F32)
        for hd in range(B_HEADS):
            ktz = jnp.where(rh == hd, kt, jnp.zeros_like(kt))
            p = (_dot(q, ktz) * dm_ref[dr, hd]).astype(BF16)
            y = jnp.where(ch == hd, _dot(p, v), y)
        s_old = s_ref[dr]
        y = y + _dot(q, s_old.astype(BF16)) * cross_ref[dr]
        kw = (kt.astype(F32) * wend_ref[dr]).astype(BF16)
        s_ref[dr] = decs_ref[...] * s_old + bd_ref[...] * _dot(kw, v)
        y_ref[...] = _group_norm(y, gm_ref[...], 1e-6).astype(y_ref.dtype)

    one_dir(0, qf_ref, ktf_ref, vf_ref, yf_ref)
    one_dir(1, qb_ref, ktb_ref, vb_ref, yb_ref)


def _retention(pn, pt, tabs, nc):
    bsz, s, _ = pn.shape
    c = ROW_TILE
    nt = s // c
    w = MXU_TILE
    dmask, wend, cross, decs, bdm, gmat = tabs
    cb = lambda i: jnp.where(i < nc, nc - 1 - i, nt - 1 - (i - nc))
    const = lambda a: pl.BlockSpec(a.shape, lambda b, i: (0,) * a.ndim)
    return pl.pallas_call(
        _ret_kernel,
        grid=(bsz, nt),
        in_specs=[pl.BlockSpec((None, c, w), lambda b, i: (b, i, 3)),
                  pl.BlockSpec((None, w, c), lambda b, i: (b, 1, i)),
                  pl.BlockSpec((None, c, w), lambda b, i: (b, i, 4)),
                  pl.BlockSpec((None, c, w), lambda b, i: (b, cb(i), 3)),
                  pl.BlockSpec((None, w, c), lambda b, i: (b, 1, cb(i))),
                  pl.BlockSpec((None, c, w), lambda b, i: (b, cb(i), 4)),
                  const(dmask), const(wend), const(cross), const(decs), const(bdm), const(gmat)],
        out_specs=[pl.BlockSpec((None, c, w), lambda b, i: (b, i, 0)),
                   pl.BlockSpec((None, c, w), lambda b, i: (b, cb(i), 0))],
        out_shape=[jax.ShapeDtypeStruct((bsz, s, w), BF16),
                   jax.ShapeDtypeStruct((bsz, s, w), BF16)],
        scratch_shapes=[pltpu.VMEM((2, w, w), F32)],
        compiler_params=_cparams(("parallel", "arbitrary")),
        name="retention",
    )(pn, pt, pn, pn, pt, pn, dmask, wend, cross, decs, bdm, gmat)


def _retention_tables(c):
    lg = jnp.asarray([math.log(1.0 - 2.0 ** (-5 - h)) for h in range(B_HEADS)], F32)
    j = jnp.arange(c, dtype=F32)
    dist = j[:, None] - j[None, :]
    df = jnp.where(dist >= 0, jnp.exp(lg[:, None, None] * jnp.maximum(dist, 0.0)), 0.0)
    dmask = jnp.stack([df, jnp.swapaxes(df, 1, 2)])
    row_head = (jnp.arange(MXU_TILE) % LANES) // 32
    col_head = jnp.arange(MXU_TILE) // 64
    wend_f = jnp.exp(lg[row_head][:, None] * (c - 1.0 - j)[None, :])
    wend_b = jnp.exp(lg[row_head][:, None] * j[None, :])
    cross_f = jnp.exp(lg[col_head][None, :] * (j + 1.0)[:, None])
    cross_b = jnp.exp(lg[col_head][None, :] * (c - j)[:, None])
    bdm = (row_head[:, None] == col_head[None, :]).astype(F32)
    decs = bdm * jnp.exp(lg * c)[col_head][None, :]
    gmat = ((col_head[:, None] == col_head[None, :]).astype(F32) / 64.0).astype(BF16)
    return (dmask, jnp.stack([wend_f, wend_b]), jnp.stack([cross_f, cross_b]), decs, bdm, gmat)


def _cd_kernel(nc, nt, seg_lens, glu_ref, glu_p, glu_n, pool_ref, pool_p, pool_n, wpool_ref,
               spool_ref, cw_ref, cb_ref, lng_ref, lnb_ref, wpw_ref, yc_ref, yd_ref, ext_ref,
               shift_ref):
    i = pl.program_id(1)
    r = pool_ref.shape[0]
    w = pool_ref.shape[1]
    prev_ok = ((i != 0) & (i != nc)).astype(F32)
    next_ok = ((i != nc - 1) & (i != nt - 1)).astype(F32)

    def fill_shifted():
        n = shift_ref.shape[1]
        for b in range(1, SUBLANES):
            shift_ref[b - 1] = ext_ref[b:b + n, :]

    def rows_at(o, n):
        a, b = divmod(o, SUBLANES)
        if b == 0:
            return ext_ref[o:o + n, :]
        return shift_ref[b - 1, SUBLANES * a:SUBLANES * a + n, :]

    x = pool_ref[...].astype(F32)
    ext_ref[0:HALO, :] = pool_p[...].astype(F32) * prev_ok
    ext_ref[HALO:HALO + r, :] = x
    ext_ref[HALO + r:, :] = pool_n[...].astype(F32) * next_ok
    fill_shifted()

    def sh(k):
        return rows_at(HALO + k, r)

    w2 = sh(-1) + x
    w4 = w2 + sh(-2) + sh(1)
    w8 = w4 + sh(-4) + sh(-3) + sh(2) + sh(3)
    w16 = w8 + sh(-8) + sh(-7) + sh(-6) + sh(-5) + sh(4) + sh(5) + sh(6) + sh(7)
    grp = lax.broadcasted_iota(jnp.int32, (r, w), 1) >> 6
    wsum = jnp.where(grp == 0, w2, jnp.where(grp == 1, w4, jnp.where(grp == 2, w8, w16)))
    half = jnp.left_shift(1, grp)
    in_ctx = i < nc
    seg_len = jnp.where(in_ctx, seg_lens[0], seg_lens[1])
    pos = lax.broadcasted_iota(jnp.int32, (r, w), 0) + (i - jnp.where(in_ctx, 0, nc)) * r
    cnt = jnp.minimum(pos + half, seg_len) - jnp.maximum(pos - half, 0)
    dlt = wsum / cnt.astype(F32) - x
    yc_ref[...] = (_dot(dlt.astype(BF16), wpool_ref[...]) * spool_ref[...]).astype(yc_ref.dtype)

    def glu(ref):
        v = ref[...].astype(F32)
        return v[:, :w] * jax.nn.sigmoid(v[:, w:])

    ext_ref[0:HALO, :] = glu(glu_p) * prev_ok
    ext_ref[HALO:HALO + r, :] = glu(glu_ref)
    ext_ref[HALO + r:, :] = glu(glu_n) * next_ok
    fill_shifted()
    sub = 64
    for r0 in range(0, r, sub):
        acc = jnp.zeros((sub, w), F32) + cb_ref[...]
        for k in range(CONV_K):
            acc = acc + rows_at(HALO - CONV_K // 2 + k + r0, sub) * cw_ref[k:k + 1, :]
        mu = jnp.mean(acc, axis=-1, keepdims=True)
        d = acc - mu
        var = jnp.mean(d * d, axis=-1, keepdims=True)
        hn = d * lax.rsqrt(var + 1e-5) * lng_ref[...] + lnb_ref[...]
        yd_ref[r0:r0 + sub, :] = _dot(jax.nn.silu(hn).astype(BF16), wpw_ref[...]).astype(yd_ref.dtype)


def _pool_conv(pn, wpool_bd, spool, conv_w, conv_b, ln_g, ln_b, w_pw, nc, seg_lens):
    bsz, s, _ = pn.shape
    r = ROW_TILE
    nt = s // r
    w = MXU_TILE
    hb = r // HALO
    nhb = s // HALO
    prev = lambda i: jnp.maximum(i * hb - 1, 0)
    nxt = lambda i: jnp.minimum((i + 1) * hb, nhb - 1)
    const = lambda a: pl.BlockSpec(a.shape, lambda b, i: (0,) * a.ndim)
    return pl.pallas_call(
        functools.partial(_cd_kernel, nc, nt, seg_lens),
        grid=(bsz, nt),
        in_specs=[pl.BlockSpec((None, r, 2 * w), lambda b, i: (b, i, 0)),
                  pl.BlockSpec((None, HALO, 2 * w), lambda b, i: (b, prev(i), 0)),
                  pl.BlockSpec((None, HALO, 2 * w), lambda b, i: (b, nxt(i), 0)),
                  pl.BlockSpec((None, r, w), lambda b, i: (b, i, 7)),
                  pl.BlockSpec((None, HALO, w), lambda b, i: (b, prev(i), 7)),
                  pl.BlockSpec((None, HALO, w), lambda b, i: (b, nxt(i), 7)),
                  const(wpool_bd), const(spool), const(conv_w), const(conv_b), const(ln_g),
                  const(ln_b), const(w_pw)],
        out_specs=[pl.BlockSpec((None, r, w), lambda b, i: (b, i, 0)),
                   pl.BlockSpec((None, r, w), lambda b, i: (b, i, 0))],
        out_shape=[jax.ShapeDtypeStruct((bsz, s, w), BF16),
                   jax.ShapeDtypeStruct((bsz, s, w), BF16)],
        scratch_shapes=[pltpu.VMEM((r + 2 * HALO, w), F32),
                        pltpu.VMEM((SUBLANES - 1, r + 2 * HALO - SUBLANES, w), F32)],
        compiler_params=_cparams(("parallel", "parallel")),
        name="pool_conv",
    )(pn, pn, pn, pn, pn, pn, wpool_bd, spool, conv_w, conv_b, ln_g, ln_b, w_pw)


def _tail_kernel(h_ref, mod_ref, gpm_ref, gprf_ref, gpof_ref, ya_ref, yf_ref, yb_ref, gf_ref, gb_ref,
                 yc_ref, yd_ref, wo_ref, wg_ref, wu_ref, wd_ref, o_ref):
    yb = (jax.nn.silu(gf_ref[...].astype(F32)) * yf_ref[...].astype(F32)
          + jax.nn.silu(gb_ref[...].astype(F32)) * yb_ref[...].astype(F32))
    ycat = jnp.concatenate([ya_ref[...], yb.astype(BF16), yc_ref[...], yd_ref[...]], axis=-1)
    y = _dot(ycat, wo_ref[...])
    x = h_ref[...] + mod_ref[2:3, :] * (_rms(y, 1e-6) * gpm_ref[...])
    u = _rms(x, 1e-6) * gprf_ref[...]
    ub = (u * (1.0 + mod_ref[4:5, :]) + mod_ref[3:4, :]).astype(BF16)
    a = (jax.nn.silu(_dot(ub, wg_ref[...])) * _dot(ub, wu_ref[...])).astype(BF16)
    f = _dot(a, wd_ref[...])
    o_ref[...] = x + mod_ref[5:6, :] * (_rms(f, 1e-6) * gpof_ref[...])


def _tail(h, mod, g_post_mix, g_pre_ffn, g_post_ffn, ya, yf, yb, pn, yc, yd, w_out, wg, wu, wd, nc):
    bsz, s, d = h.shape
    r = ROW_TILE
    nt = s // r
    w = MXU_TILE
    msel = lambda b, i: (jnp.where(i < nc, bsz, b), 0, 0)
    tile = lambda col: pl.BlockSpec((None, r, w), lambda b, i: (b, i, col))
    vec = pl.BlockSpec((1, d), lambda b, i: (0, 0))
    resident = lambda a: pl.BlockSpec(a.shape, lambda b, i: (0, 0), pipeline_mode=pl.Buffered(1))
    return pl.pallas_call(
        _tail_kernel,
        grid=(bsz, nt),
        in_specs=[pl.BlockSpec((None, r, d), lambda b, i: (b, i, 0)),
                  pl.BlockSpec((None, 6, d), msel),
                  vec, vec, vec,
                  tile(0), tile(0), tile(0), tile(5), tile(6), tile(0), tile(0),
                  resident(w_out), resident(wg), resident(wu), resident(wd)],
        out_specs=pl.BlockSpec((None, r, d), lambda b, i: (b, i, 0)),
        out_shape=jax.ShapeDtypeStruct((bsz, s, d), F32),
        compiler_params=_cparams(("parallel", "parallel")),
        name="out_ffn",
    )(h, mod, g_post_mix, g_pre_ffn, g_post_ffn, ya, yf, yb, pn, pn, yc, yd, w_out, wg, wu, wd)


def _rope_tables(t_len, ctx_len, dqk_a, dk_b):
    rows = t_len // GRID_W
    row = jnp.repeat(jnp.arange(rows, dtype=F32), GRID_W)
    col = jnp.tile(jnp.arange(GRID_W, dtype=F32), rows)

    def cs(d, reps):
        n_freq = d // 4
        inv = ROPE_BASE ** (-jnp.arange(n_freq, dtype=F32) / n_freq)
        ang = jnp.concatenate([row[:, None] * inv, col[:, None] * inv], axis=-1)
        cos = jnp.concatenate([jnp.ones((ctx_len, d // 2), F32), jnp.cos(ang)], axis=0)
        sin = jnp.concatenate([jnp.zeros((ctx_len, d // 2), F32), jnp.sin(ang)], axis=0)
        return jnp.tile(cos, (1, reps)), jnp.tile(sin, (1, reps))

    ca, sa = cs(dqk_a, LANES // (dqk_a // 2))
    cb, sb = cs(dk_b, LANES // (dk_b // 2))
    tab_n = jnp.concatenate([ca, sa, cb, sb], axis=1)
    return tab_n, tab_n.T


def _rotary_perm(n_groups, d):
    half, g, i = jnp.meshgrid(jnp.arange(2), jnp.arange(n_groups), jnp.arange(d // 2), indexing="ij")
    return (g * d + half * (d // 2) + i).reshape(-1)


def kernel(x, c, ctx, c_ctx, w_ada, b_ada, g_pre_mix, g_post_mix, g_pre_ffn, g_post_ffn, w_in, w_out, lam_q1, lam_k1, lam_q2, lam_k2, g_subln, w_pool, s_pool, conv_w, conv_b, conv_ln_g, conv_ln_b, w_conv_out, w_ffn_gate, w_ffn_up, w_ffn_down):
    bsz, t_len, d = x.shape
    ctx_len = ctx.shape[1]
    depth = w_in.shape[0]
    assert d == 8 * A_HEADS * 32 and t_len % ROW_TILE == 0 and ctx_len % ROW_TILE == 0
    assert t_len % GRID_W == 0 and bsz < 8
    qw = d // 4
    dqk_a = qw // (2 * A_HEADS)
    dk_b = qw // B_HEADS
    nc = ctx_len // ROW_TILE

    h = jnp.concatenate([ctx, x], axis=1)

    cond = jnp.zeros((8, d), F32).at[:bsz].set(c).at[bsz].set(c_ctx)
    mods = _modulation(cond, w_ada, b_ada).reshape(depth, 8, 6, d)

    pa = _rotary_perm(2 * A_HEADS, dqk_a)
    pb = _rotary_perm(B_HEADS, dk_b)
    col = lambda k: w_in[:, :, k * qw:(k + 1) * qw]
    k_a, v_a, k_b, v_b, q_a, q_b, g_f, g_b, pool = (col(k) for k in range(9))
    glu = w_in[:, :, 9 * qw:]
    w_nat = jnp.concatenate([glu, k_a[:, :, pa], q_b[:, :, pb], v_b, g_f, g_b, pool], axis=-1).astype(BF16)
    w_tr = jnp.swapaxes(jnp.concatenate([q_a[:, :, pa], k_b[:, :, pb], v_a], axis=-1), 1, 2).astype(BF16)
    w_out_b = w_out.astype(BF16)
    wg_b, wu_b, wd_b = w_ffn_gate.astype(BF16), w_ffn_up.astype(BF16), w_ffn_down.astype(BF16)
    w_pw_b = w_conv_out.astype(BF16)
    eye = jnp.eye(len(POOL_WINDOWS), dtype=F32)
    wpool_bd = jnp.einsum("lgce,gh->lgche", w_pool, eye).reshape(depth, qw, qw).astype(BF16)
    conv_w2 = jnp.pad(conv_w.reshape(depth, CONV_K, qw), ((0, 0), (0, 1), (0, 0)))
    lamv = jnp.stack([lam_q1, lam_k1, lam_q2, lam_k2], axis=1)
    g_col = jnp.tile(g_subln, (1, 1))[:, :, None]

    tab_n, tab_t = _rope_tables(t_len, ctx_len, dqk_a, dk_b)
    ret_tabs = _retention_tables(ROW_TILE)
    row = lambda a, l: a[l][None, :]

    for l in range(depth):
        lam_init = 0.8 - 0.6 * math.exp(-0.3 * l)
        mod = mods[l]
        pn, pt = _inproj(h, mod, row(g_pre_mix, l), w_nat[l], w_tr[l], tab_n, tab_t, nc,
                         dqk_a ** -0.5 * LOG2_E, dk_b ** -0.5)
        k_lat = pn[:, ctx_len:, 2 * qw:3 * qw]
        vt_lat = pt[:, 2 * qw:3 * qw, ctx_len:]
        ya = _attention(pn, pt, k_lat, vt_lat, lamv[l], g_col[l], nc, lam_init)
        yf, yb = _retention(pn, pt, ret_tabs, nc)
        yc, yd = _pool_conv(pn, wpool_bd[l], row(s_pool, l), conv_w2[l], row(conv_b, l),
                            row(conv_ln_g, l), row(conv_ln_b, l), w_pw_b[l], nc, (ctx_len, t_len))
        h = _tail(h, mod, row(g_post_mix, l), row(g_pre_ffn, l), row(g_post_ffn, l), ya, yf, yb, pn,
                  yc, yd, w_out_b[l], wg_b[l], wu_b[l], wd_b[l], nc)
    return h[:, ctx_len:, :]
```

```python
import functools
import math

import jax
import jax.numpy as jnp
from jax import lax
from jax.experimental import pallas as pl
from jax.experimental.pallas import tpu as pltpu

F32 = jnp.float32
BF16 = jnp.bfloat16

GRID_W = 64
ROPE_BASE = 10000.0
A_HEADS = 4
B_HEADS = 4
POOL_WINDOWS = (2, 4, 8, 16)
CONV_K = 31
LOG2_E = math.log2(math.e)

LANES = 128
SUBLANES = 8
MXU_TILE = 256
ROW_TILE = 256
HALO = 16
VMEM_LIMIT = 56 * 1024 * 1024
ATTN_KEY_TILE = 2048
ATTN_SKEW = 3
ATTN_HEADROOM = 8.0
ATTN_SUM_ROWS = 16


def _cparams(sem):
    return pltpu.CompilerParams(dimension_semantics=sem, vmem_limit_bytes=VMEM_LIMIT)


def _rms(x, eps):
    return x * lax.rsqrt(jnp.mean(x * x, axis=-1, keepdims=True) + eps)


def _dot(a, b):
    return jnp.dot(a, b, preferred_element_type=F32)


def _mod_kernel(c_ref, w_ref, b_ref, o_ref):
    a = jax.nn.silu(c_ref[...])
    o_ref[...] = jnp.dot(a, w_ref[...], preferred_element_type=F32,
                         precision=lax.Precision.HIGHEST) + b_ref[...]


def _modulation(cond, w_ada, b_ada):
    n_layers, d, n6 = w_ada.shape
    tn = 2048
    return pl.pallas_call(
        _mod_kernel,
        grid=(n_layers, n6 // tn),
        in_specs=[pl.BlockSpec((8, d), lambda l, j: (0, 0)),
                  pl.BlockSpec((None, d, tn), lambda l, j: (l, 0, j)),
                  pl.BlockSpec((None, 1, tn), lambda l, j: (l, 0, j))],
        out_specs=pl.BlockSpec((None, 8, tn), lambda l, j: (l, 0, j)),
        out_shape=jax.ShapeDtypeStruct((n_layers, 8, n6), F32),
        compiler_params=_cparams(("parallel", "parallel")),
        name="adaln_mod",
    )(cond, w_ada, b_ada.reshape(n_layers, 1, n6))


def _inproj_kernel(a_scale, b_scale, h_ref, mod_ref, g_ref, wn_ref, wt_ref, tn_ref, tt_ref,
                   pn_ref, pt_ref):
    mod = mod_ref[...]
    u = _rms(h_ref[...], 1e-6) * g_ref[...]
    ub = (u * (1.0 + mod[1:2]) + mod[0:1]).astype(BF16)

    def nat(c0, c1):
        return _dot(ub, wn_ref[:, c0:c1])

    def tr(r0, r1):
        return lax.dot_general(wt_ref[r0:r1, :], ub, (((1,), (1,)), ((), ())),
                               preferred_element_type=F32)

    pn_ref[:, 0:512] = nat(0, 512).astype(BF16)
    ka = nat(512, 768)
    x1, x2 = ka[:, :LANES], ka[:, LANES:]
    cs, sn = tn_ref[:, 0:128], tn_ref[:, 128:256]
    pn_ref[:, 512:640] = (x1 * cs - x2 * sn).astype(BF16)
    pn_ref[:, 640:768] = (x1 * sn + x2 * cs).astype(BF16)
    qb = nat(768, 1024)
    x1, x2 = qb[:, :LANES], qb[:, LANES:]
    cs, sn = tn_ref[:, 256:384], tn_ref[:, 384:512]
    pn_ref[:, 768:896] = (x1 * cs - x2 * sn).astype(BF16)
    pn_ref[:, 896:1024] = (x1 * sn + x2 * cs).astype(BF16)
    pn_ref[:, 1024:2048] = nat(1024, 2048).astype(BF16)

    qa = tr(0, 256) * a_scale
    x1, x2 = qa[:LANES], qa[LANES:]
    cs, sn = tt_ref[0:128, :], tt_ref[128:256, :]
    pt_ref[0:128, :] = (x1 * cs - x2 * sn).astype(BF16)
    pt_ref[128:256, :] = (x1 * sn + x2 * cs).astype(BF16)
    kb = tr(256, 512) * b_scale
    x1, x2 = kb[:LANES], kb[LANES:]
    cs, sn = tt_ref[256:384, :], tt_ref[384:512, :]
    pt_ref[256:384, :] = (x1 * cs - x2 * sn).astype(BF16)
    pt_ref[384:512, :] = (x1 * sn + x2 * cs).astype(BF16)
    pt_ref[512:768, :] = tr(512, 768).astype(BF16)


def _inproj(h, mod, g_pre, w_nat, w_tr, tab_n, tab_t, nc, a_scale, b_scale):
    bsz, s, d = h.shape
    nt = s // ROW_TILE
    nn, ntr = w_nat.shape[1], w_tr.shape[0]
    msel = lambda b, i: (jnp.where(i < nc, bsz, b), 0, 0)
    return pl.pallas_call(
        functools.partial(_inproj_kernel, a_scale, b_scale),
        grid=(bsz, nt),
        in_specs=[pl.BlockSpec((None, ROW_TILE, d), lambda b, i: (b, i, 0)),
                  pl.BlockSpec((None, 6, d), msel),
                  pl.BlockSpec((1, d), lambda b, i: (0, 0)),
                  pl.BlockSpec((d, nn), lambda b, i: (0, 0)),
                  pl.BlockSpec((ntr, d), lambda b, i: (0, 0)),
                  pl.BlockSpec((ROW_TILE, 4 * LANES), lambda b, i: (i, 0)),
                  pl.BlockSpec((4 * LANES, ROW_TILE), lambda b, i: (0, i))],
        out_specs=[pl.BlockSpec((None, ROW_TILE, nn), lambda b, i: (b, i, 0)),
                   pl.BlockSpec((None, ntr, ROW_TILE), lambda b, i: (b, 0, i))],
        out_shape=[jax.ShapeDtypeStruct((bsz, s, nn), BF16),
                   jax.ShapeDtypeStruct((bsz, ntr, s), BF16)],
        compiler_params=_cparams(("parallel", "parallel")),
        name="in_proj",
    )(h, mod, g_pre, w_nat, w_tr, tab_n, tab_t)


def _attn_kernel(nc, n_ksteps, lam_init, qt_ref, kc_ref, vtc_ref, kl_ref, vtl_ref, lamv_ref, g_ref,
                 o_ref, qz_ref, m_ref, acc_ref, tmax_ref, pv_ref):
    qi = pl.program_id(1)
    ki = pl.program_id(2)
    n_groups = 2 * A_HEADS
    dv = vtc_ref.shape[0] // A_HEADS

    def attend(k_ref, vt_ref):
        k = k_ref[...]
        ones = jnp.ones((ATTN_SUM_ROWS, k.shape[0]), BF16)
        s_q = [_dot(k, qz_ref[g]) for g in range(ATTN_SKEW)]
        for g in range(n_groups):
            hd = g // 2
            s = s_q.pop(0)
            if g + ATTN_SKEW < n_groups:
                s_q.append(_dot(k, qz_ref[g + ATTN_SKEW]))
            m_old = m_ref[g]
            m_new = jnp.maximum(m_old, jnp.max(s, axis=0, keepdims=True))
            p = jnp.exp2((s - m_new).astype(BF16))
            v_ext = jnp.concatenate([vt_ref[hd * dv:(hd + 1) * dv, :], ones], axis=0)
            acc_ref[g] = jnp.exp2(m_old - m_new) * acc_ref[g] + _dot(v_ext, p)
            m_ref[g] = m_new

    def attend_lagged(k_ref, vt_ref):
        k = k_ref[...]
        ones = jnp.ones((ATTN_SUM_ROWS, k.shape[0]), BF16)
        s_q = [_dot(k, qz_ref[g]) for g in range(ATTN_SKEW)]
        excess = None
        for g in range(n_groups):
            hd = g // 2
            s = s_q.pop(0)
            if g + ATTN_SKEW < n_groups:
                s_q.append(_dot(k, qz_ref[g + ATTN_SKEW]))
            m_fix = m_ref[g]
            p = jnp.exp2((s - m_fix).astype(BF16))
            t_max = jnp.max(s, axis=0, keepdims=True)
            tmax_ref[g] = t_max
            excess = t_max - m_fix if excess is None else jnp.maximum(excess, t_max - m_fix)
            v_ext = jnp.concatenate([vt_ref[hd * dv:(hd + 1) * dv, :], ones], axis=0)
            pv_ref[g] = _dot(v_ext, p)
        return jnp.max(excess)

    def commit_lagged():
        for g in range(n_groups):
            m_old = m_ref[g]
            m_new = jnp.maximum(m_old, tmax_ref[g])
            acc_ref[g] = (acc_ref[g] + pv_ref[g]) * jnp.exp2(m_old - m_new)
            m_ref[g] = m_new

    @pl.when(ki == 0)
    def _first():
        qt = qt_ref[...]
        rg = (lax.broadcasted_iota(jnp.int32, qt.shape, 0) & (LANES - 1)) >> 4
        for g in range(n_groups):
            qz_ref[g] = jnp.where(rg == g, qt, jnp.zeros_like(qt))
        m_ref[...] = jnp.full(m_ref.shape, -1e30, F32)
        acc_ref[...] = jnp.zeros(acc_ref.shape, F32)
        attend(kc_ref, vtc_ref)

    @pl.when((ki > 0) & (qi >= nc))
    def _latent():
        within = attend_lagged(kl_ref, vtl_ref) <= ATTN_HEADROOM

        @pl.when(within)
        def _commit():
            commit_lagged()

        @pl.when(jnp.logical_not(within))
        def _redo():
            attend(kl_ref, vtl_ref)

    @pl.when(ki == n_ksteps - 1)
    def _fin():
        lv = lamv_ref[...]
        a1 = jnp.sum(lv[0:1] * lv[1:2], axis=-1, keepdims=True)
        a2 = jnp.sum(lv[2:3] * lv[3:4], axis=-1, keepdims=True)
        lam = jnp.exp(a1) - jnp.exp(a2) + lam_init
        outs = []
        for hd in range(A_HEADS):
            a_pos, a_neg = acc_ref[2 * hd], acc_ref[2 * hd + 1]
            o = a_pos[:dv] / a_pos[dv:dv + 1] - lam * (a_neg[:dv] / a_neg[dv:dv + 1])
            ms = jnp.mean(o * o, axis=0, keepdims=True)
            outs.append(o * lax.rsqrt(ms + 1e-5) * g_ref[...] * (1.0 - lam_init))
        o_ref[...] = jnp.concatenate(outs, axis=0).T.astype(o_ref.dtype)


def _attention(pn, pt, ctx_len, lamv, g_col, nc, lam_init):
    bsz, s, _ = pn.shape
    t_len = s - ctx_len
    nq = s // ROW_TILE
    tk = min(ATTN_KEY_TILE, t_len)
    assert t_len % tk == 0
    n_ksteps = 1 + t_len // tk
    w = MXU_TILE
    dv_ext = w // A_HEADS + ATTN_SUM_ROWS
    kl = lambda qi, ki: pl.multiple_of(ctx_len + jnp.where(qi < nc, 0, jnp.maximum(ki - 1, 0)) * tk, ROW_TILE)
    return pl.pallas_call(
        functools.partial(_attn_kernel, nc, n_ksteps, lam_init),
        grid=(bsz, nq, n_ksteps),
        in_specs=[pl.BlockSpec((None, w, ROW_TILE), lambda b, qi, ki: (b, 0, qi)),
                  pl.BlockSpec((None, ctx_len, w), lambda b, qi, ki: (b, 0, 2)),
                  pl.BlockSpec((None, w, ctx_len), lambda b, qi, ki: (b, 2, 0)),
                  pl.BlockSpec((None, pl.Element(tk), pl.Element(w)), lambda b, qi, ki: (b, kl(qi, ki), 2 * w)),
                  pl.BlockSpec((None, pl.Element(w), pl.Element(tk)), lambda b, qi, ki: (b, 2 * w, kl(qi, ki))),
                  pl.BlockSpec(lamv.shape, lambda b, qi, ki: (0, 0)),
                  pl.BlockSpec(g_col.shape, lambda b, qi, ki: (0, 0))],
        out_specs=pl.BlockSpec((None, ROW_TILE, w), lambda b, qi, ki: (b, qi, 0)),
        out_shape=jax.ShapeDtypeStruct((bsz, s, w), BF16),
        scratch_shapes=[pltpu.VMEM((2 * A_HEADS, w, ROW_TILE), BF16),
                        pltpu.VMEM((2 * A_HEADS, 1, ROW_TILE), F32),
                        pltpu.VMEM((2 * A_HEADS, dv_ext, ROW_TILE), F32),
                        pltpu.VMEM((2 * A_HEADS, 1, ROW_TILE), F32),
                        pltpu.VMEM((2 * A_HEADS, dv_ext, ROW_TILE), F32)],
        compiler_params=_cparams(("parallel", "parallel", "arbitrary")),
        name="diff_attn",
    )(pt, pn, pt, pn, pt, lamv, g_col)


def _group_norm(y, gmat, eps):
    yh = y.astype(BF16)
    yl = (y - yh.astype(F32)).astype(BF16)
    d = y - (_dot(yh, gmat) + _dot(yl, gmat))
    return d * lax.rsqrt(_dot((d * d).astype(BF16), gmat) + eps)


def _ret_kernel(qf_ref, ktf_ref, vf_ref, qb_ref, ktb_ref, vb_ref, dm_ref, wend_ref, cross_ref,
                decs_ref, bd_ref, gm_ref, yf_ref, yb_ref, s_ref):
    i = pl.program_id(1)

    @pl.when(i == 0)
    def _init():
        s_ref[...] = jnp.zeros(s_ref.shape, F32)

    def one_dir(dr, q_ref, kt_ref, v_ref, y_ref):
        q = q_ref[...]
        kt = kt_ref[...]
        v = v_ref[...]
        rh = (lax.broadcasted_iota(jnp.int32, kt.shape, 0) & (LANES - 1)) >> 5
        ch = lax.broadcasted_iota(jnp.int32, (q.shape[0], v.shape[1]), 1) >> 6
        y = jnp.zeros((q.shape[0], v.shape[1]), F32)
        for hd in range(B_HEADS):
            ktz = jnp.where(rh == hd, kt, jnp.zeros_like(kt))
            p = (_dot(q, ktz) * dm_ref[dr, hd]).astype(BF16)
            y = jnp.where(ch == hd, _dot(p, v), y)
        s_old = s_ref[dr]
        y = y + _dot(q, s_old.astype(BF16)) * cross_ref[dr]
        kw = (kt.astype(F32) * wend_ref[dr]).astype(BF16)
        s_ref[dr] = decs_ref[...] * s_old + bd_ref[...] * _dot(kw, v)
        y_ref[...] = _group_norm(y, gm_ref[...], 1e-6).astype(y_ref.dtype)

    one_dir(0, qf_ref, ktf_ref, vf_ref, yf_ref)
    one_dir(1, qb_ref, ktb_ref, vb_ref, yb_ref)


def _retention(pn, pt, tabs, nc):
    bsz, s, _ = pn.shape
    c = ROW_TILE
    nt = s // c
    w = MXU_TILE
    dmask, wend, cross, decs, bdm, gmat = tabs
    cb = lambda i: jnp.where(i < nc, nc - 1 - i, nt - 1 - (i - nc))
    const = lambda a: pl.BlockSpec(a.shape, lambda b, i: (0,) * a.ndim)
    return pl.pallas_call(
        _ret_kernel,
        grid=(bsz, nt),
        in_specs=[pl.BlockSpec((None, c, w), lambda b, i: (b, i, 3)),
                  pl.BlockSpec((None, w, c), lambda b, i: (b, 1, i)),
                  pl.BlockSpec((None, c, w), lambda b, i: (b, i, 4)),
                  pl.BlockSpec((None, c, w), lambda b, i: (b, cb(i), 3)),
                  pl.BlockSpec((None, w, c), lambda b, i: (b, 1, cb(i))),
                  pl.BlockSpec((None, c, w), lambda b, i: (b, cb(i), 4)),
                  const(dmask), const(wend), const(cross), const(decs), const(bdm), const(gmat)],
        out_specs=[pl.BlockSpec((None, c, w), lambda b, i: (b, i, 0)),
                   pl.BlockSpec((None, c, w), lambda b, i: (b, cb(i), 0))],
        out_shape=[jax.ShapeDtypeStruct((bsz, s, w), BF16),
                   jax.ShapeDtypeStruct((bsz, s, w), BF16)],
        scratch_shapes=[pltpu.VMEM((2, w, w), F32)],
        compiler_params=_cparams(("parallel", "arbitrary")),
        name="retention",
    )(pn, pt, pn, pn, pt, pn, dmask, wend, cross, decs, bdm, gmat)


def _retention_tables(c):
    lg = jnp.asarray([math.log(1.0 - 2.0 ** (-5 - h)) for h in range(B_HEADS)], F32)
    j = jnp.arange(c, dtype=F32)
    dist = j[:, None] - j[None, :]
    df = jnp.where(dist >= 0, jnp.exp(lg[:, None, None] * jnp.maximum(dist, 0.0)), 0.0)
    dmask = jnp.stack([df, jnp.swapaxes(df, 1, 2)])
    row_head = (jnp.arange(MXU_TILE) % LANES) // 32
    col_head = jnp.arange(MXU_TILE) // 64
    wend_f = jnp.exp(lg[row_head][:, None] * (c - 1.0 - j)[None, :])
    wend_b = jnp.exp(lg[row_head][:, None] * j[None, :])
    cross_f = jnp.exp(lg[col_head][None, :] * (j + 1.0)[:, None])
    cross_b = jnp.exp(lg[col_head][None, :] * (c - j)[:, None])
    bdm = (row_head[:, None] == col_head[None, :]).astype(F32)
    decs = bdm * jnp.exp(lg * c)[col_head][None, :]
    gmat = ((col_head[:, None] == col_head[None, :]).astype(F32) / 64.0).astype(BF16)
    return (dmask, jnp.stack([wend_f, wend_b]), jnp.stack([cross_f, cross_b]), decs, bdm, gmat)


def _cd_kernel(nc, nt, seg_lens, glu_ref, glu_p, glu_n, pool_ref, pool_p, pool_n, wpool_ref,
               spool_ref, cw_ref, cb_ref, lng_ref, lnb_ref, wpw_ref, yc_ref, yd_ref, ext_ref,
               shift_ref):
    i = pl.program_id(1)
    r = pool_ref.shape[0]
    w = pool_ref.shape[1]
    prev_ok = ((i != 0) & (i != nc)).astype(F32)
    next_ok = ((i != nc - 1) & (i != nt - 1)).astype(F32)

    def fill_shifted():
        n = shift_ref.shape[1]
        for b in range(1, SUBLANES):
            shift_ref[b - 1] = ext_ref[b:b + n, :]

    def rows_at(o, n):
        a, b = divmod(o, SUBLANES)
        if b == 0:
            return ext_ref[o:o + n, :]
        return shift_ref[b - 1, SUBLANES * a:SUBLANES * a + n, :]

    x = pool_ref[...].astype(F32)
    ext_ref[0:HALO, :] = pool_p[...].astype(F32) * prev_ok
    ext_ref[HALO:HALO + r, :] = x
    ext_ref[HALO + r:, :] = pool_n[...].astype(F32) * next_ok
    fill_shifted()

    def sh(k):
        return rows_at(HALO + k, r)

    w2 = sh(-1) + x
    w4 = w2 + sh(-2) + sh(1)
    w8 = w4 + sh(-4) + sh(-3) + sh(2) + sh(3)
    w16 = w8 + sh(-8) + sh(-7) + sh(-6) + sh(-5) + sh(4) + sh(5) + sh(6) + sh(7)
    grp = lax.broadcasted_iota(jnp.int32, (r, w), 1) >> 6
    wsum = jnp.where(grp == 0, w2, jnp.where(grp == 1, w4, jnp.where(grp == 2, w8, w16)))
    half = jnp.left_shift(1, grp)
    in_ctx = i < nc
    seg_len = jnp.where(in_ctx, seg_lens[0], seg_lens[1])
    pos = lax.broadcasted_iota(jnp.int32, (r, w), 0) + (i - jnp.where(in_ctx, 0, nc)) * r
    cnt = jnp.minimum(pos + half, seg_len) - jnp.maximum(pos - half, 0)
    dlt = wsum / cnt.astype(F32) - x
    yc_ref[...] = (_dot(dlt.astype(BF16), wpool_ref[...]) * spool_ref[...]).astype(yc_ref.dtype)

    def glu(ref):
        v = ref[...].astype(F32)
        return v[:, :w] * jax.nn.sigmoid(v[:, w:])

    ext_ref[0:HALO, :] = glu(glu_p) * prev_ok
    ext_ref[HALO:HALO + r, :] = glu(glu_ref)
    ext_ref[HALO + r:, :] = glu(glu_n) * next_ok
    fill_shifted()
    sub = 64
    for r0 in range(0, r, sub):
        acc = jnp.zeros((sub, w), F32) + cb_ref[...]
        for k in range(CONV_K):
            acc = acc + rows_at(HALO - CONV_K // 2 + k + r0, sub) * cw_ref[k:k + 1, :]
        mu = jnp.mean(acc, axis=-1, keepdims=True)
        d = acc - mu
        var = jnp.mean(d * d, axis=-1, keepdims=True)
        hn = d * lax.rsqrt(var + 1e-5) * lng_ref[...] + lnb_ref[...]
        yd_ref[r0:r0 + sub, :] = _dot(jax.nn.silu(hn).astype(BF16), wpw_ref[...]).astype(yd_ref.dtype)


def _pool_conv(pn, wpool_bd, spool, conv_w, conv_b, ln_g, ln_b, w_pw, nc, seg_lens):
    bsz, s, _ = pn.shape
    r = ROW_TILE
    nt = s // r
    w = MXU_TILE
    hb = r // HALO
    nhb = s // HALO
    prev = lambda i: jnp.maximum(i * hb - 1, 0)
    nxt = lambda i: jnp.minimum((i + 1) * hb, nhb - 1)
    const = lambda a: pl.BlockSpec(a.shape, lambda b, i: (0,) * a.ndim)
    return pl.pallas_call(
        functools.partial(_cd_kernel, nc, nt, seg_lens),
        grid=(bsz, nt),
        in_specs=[pl.BlockSpec((None, r, 2 * w), lambda b, i: (b, i, 0)),
                  pl.BlockSpec((None, HALO, 2 * w), lambda b, i: (b, prev(i), 0)),
                  pl.BlockSpec((None, HALO, 2 * w), lambda b, i: (b, nxt(i), 0)),
                  pl.BlockSpec((None, r, w), lambda b, i: (b, i, 7)),
                  pl.BlockSpec((None, HALO, w), lambda b, i: (b, prev(i), 7)),
                  pl.BlockSpec((None, HALO, w), lambda b, i: (b, nxt(i), 7)),
                  const(wpool_bd), const(spool), const(conv_w), const(conv_b), const(ln_g),
                  const(ln_b), const(w_pw)],
        out_specs=[pl.BlockSpec((None, r, w), lambda b, i: (b, i, 0)),
                   pl.BlockSpec((None, r, w), lambda b, i: (b, i, 0))],
        out_shape=[jax.ShapeDtypeStruct((bsz, s, w), BF16),
                   jax.ShapeDtypeStruct((bsz, s, w), BF16)],
        scratch_shapes=[pltpu.VMEM((r + 2 * HALO, w), F32),
                        pltpu.VMEM((SUBLANES - 1, r + 2 * HALO - SUBLANES, w), F32)],
        compiler_params=_cparams(("parallel", "parallel")),
        name="pool_conv",
    )(pn, pn, pn, pn, pn, pn, wpool_bd, spool, conv_w, conv_b, ln_g, ln_b, w_pw)


def _tail_kernel(h_ref, mod_ref, gpm_ref, gprf_ref, gpof_ref, ya_ref, yf_ref, yb_ref, gf_ref, gb_ref,
                 yc_ref, yd_ref, wo_ref, wg_ref, wu_ref, wd_ref, o_ref):
    yb = (jax.nn.silu(gf_ref[...].astype(F32)) * yf_ref[...].astype(F32)
          + jax.nn.silu(gb_ref[...].astype(F32)) * yb_ref[...].astype(F32))
    ycat = jnp.concatenate([ya_ref[...], yb.astype(BF16), yc_ref[...], yd_ref[...]], axis=-1)
    y = _dot(ycat, wo_ref[...])
    x = h_ref[...] + mod_ref[2:3, :] * (_rms(y, 1e-6) * gpm_ref[...])
    u = _rms(x, 1e-6) * gprf_ref[...]
    ub = (u * (1.0 + mod_ref[4:5, :]) + mod_ref[3:4, :]).astype(BF16)
    a = (jax.nn.silu(_dot(ub, wg_ref[...])) * _dot(ub, wu_ref[...])).astype(BF16)
    f = _dot(a, wd_ref[...])
    o_ref[...] = x + mod_ref[5:6, :] * (_rms(f, 1e-6) * gpof_ref[...])


def _tail(h, mod, g_post_mix, g_pre_ffn, g_post_ffn, ya, yf, yb, pn, yc, yd, w_out, wg, wu, wd, nc,
          latent_only):
    bsz, s, d = h.shape
    r = ROW_TILE
    nt = s // r
    out_rows = s - nc * r if latent_only else s
    out_tile = (lambda b, i: (b, jnp.maximum(i - nc, 0), 0)) if latent_only else (lambda b, i: (b, i, 0))
    w = MXU_TILE
    msel = lambda b, i: (jnp.where(i < nc, bsz, b), 0, 0)
    tile = lambda col: pl.BlockSpec((None, r, w), lambda b, i: (b, i, col))
    vec = pl.BlockSpec((1, d), lambda b, i: (0, 0))
    resident = lambda a: pl.BlockSpec(a.shape, lambda b, i: (0, 0), pipeline_mode=pl.Buffered(1))
    return pl.pallas_call(
        _tail_kernel,
        grid=(bsz, nt),
        in_specs=[pl.BlockSpec((None, r, d), lambda b, i: (b, i, 0)),
                  pl.BlockSpec((None, 6, d), msel),
                  vec, vec, vec,
                  tile(0), tile(0), tile(0), tile(5), tile(6), tile(0), tile(0),
                  resident(w_out), resident(wg), resident(wu), resident(wd)],
        out_specs=pl.BlockSpec((None, r, d), out_tile),
        out_shape=jax.ShapeDtypeStruct((bsz, out_rows, d), F32),
        compiler_params=_cparams(("parallel", "arbitrary")),
        name="out_ffn",
    )(h, mod, g_post_mix, g_pre_ffn, g_post_ffn, ya, yf, yb, pn, pn, yc, yd, w_out, wg, wu, wd)


def _rope_tables(t_len, ctx_len, dqk_a, dk_b):
    rows = t_len // GRID_W
    row = jnp.repeat(jnp.arange(rows, dtype=F32), GRID_W)
    col = jnp.tile(jnp.arange(GRID_W, dtype=F32), rows)

    def cs(d, reps):
        n_freq = d // 4
        inv = ROPE_BASE ** (-jnp.arange(n_freq, dtype=F32) / n_freq)
        ang = jnp.concatenate([row[:, None] * inv, col[:, None] * inv], axis=-1)
        cos = jnp.concatenate([jnp.ones((ctx_len, d // 2), F32), jnp.cos(ang)], axis=0)
        sin = jnp.concatenate([jnp.zeros((ctx_len, d // 2), F32), jnp.sin(ang)], axis=0)
        return jnp.tile(cos, (1, reps)), jnp.tile(sin, (1, reps))

    ca, sa = cs(dqk_a, LANES // (dqk_a // 2))
    cb, sb = cs(dk_b, LANES // (dk_b // 2))
    tab_n = jnp.concatenate([ca, sa, cb, sb], axis=1)
    return tab_n, tab_n.T


def _rotary_perm(n_groups, d):
    half, g, i = jnp.meshgrid(jnp.arange(2), jnp.arange(n_groups), jnp.arange(d // 2), indexing="ij")
    return (g * d + half * (d // 2) + i).reshape(-1)


def kernel(x, c, ctx, c_ctx, w_ada, b_ada, g_pre_mix, g_post_mix, g_pre_ffn, g_post_ffn, w_in, w_out, lam_q1, lam_k1, lam_q2, lam_k2, g_subln, w_pool, s_pool, conv_w, conv_b, conv_ln_g, conv_ln_b, w_conv_out, w_ffn_gate, w_ffn_up, w_ffn_down):
    bsz, t_len, d = x.shape
    ctx_len = ctx.shape[1]
    depth = w_in.shape[0]
    assert d == 8 * A_HEADS * 32 and t_len % ROW_TILE == 0 and ctx_len % ROW_TILE == 0
    assert t_len % GRID_W == 0 and bsz < 8
    qw = d // 4
    dqk_a = qw // (2 * A_HEADS)
    dk_b = qw // B_HEADS
    nc = ctx_len // ROW_TILE

    h = jnp.concatenate([ctx, x], axis=1)

    cond = jnp.zeros((8, d), F32).at[:bsz].set(c).at[bsz].set(c_ctx)
    mods = _modulation(cond, w_ada, b_ada).reshape(depth, 8, 6, d)

    pa = _rotary_perm(2 * A_HEADS, dqk_a)
    pb = _rotary_perm(B_HEADS, dk_b)
    col = lambda k: w_in[:, :, k * qw:(k + 1) * qw]
    k_a, v_a, k_b, v_b, q_a, q_b, g_f, g_b, pool = (col(k) for k in range(9))
    glu = w_in[:, :, 9 * qw:]
    w_nat = jnp.concatenate([glu, k_a[:, :, pa], q_b[:, :, pb], v_b, g_f, g_b, pool], axis=-1).astype(BF16)
    w_tr = jnp.swapaxes(jnp.concatenate([q_a[:, :, pa], k_b[:, :, pb], v_a], axis=-1), 1, 2).astype(BF16)
    w_out_b = w_out.astype(BF16)
    wg_b, wu_b, wd_b = w_ffn_gate.astype(BF16), w_ffn_up.astype(BF16), w_ffn_down.astype(BF16)
    w_pw_b = w_conv_out.astype(BF16)
    eye = jnp.eye(len(POOL_WINDOWS), dtype=F32)
    wpool_bd = jnp.einsum("lgce,gh->lgche", w_pool, eye).reshape(depth, qw, qw).astype(BF16)
    conv_w2 = jnp.pad(conv_w.reshape(depth, CONV_K, qw), ((0, 0), (0, 1), (0, 0)))
    lamv = jnp.stack([lam_q1, lam_k1, lam_q2, lam_k2], axis=1)
    g_col = jnp.tile(g_subln, (1, 1))[:, :, None]

    tab_n, tab_t = _rope_tables(t_len, ctx_len, dqk_a, dk_b)
    ret_tabs = _retention_tables(ROW_TILE)
    row = lambda a, l: a[l][None, :]

    for l in range(depth):
        lam_init = 0.8 - 0.6 * math.exp(-0.3 * l)
        mod = mods[l]
        pn, pt = _inproj(h, mod, row(g_pre_mix, l), w_nat[l], w_tr[l], tab_n, tab_t, nc,
                         dqk_a ** -0.5 * LOG2_E, dk_b ** -0.5)
        ya = _attention(pn, pt, ctx_len, lamv[l], g_col[l], nc, lam_init)
        yf, yb = _retention(pn, pt, ret_tabs, nc)
        yc, yd = _pool_conv(pn, wpool_bd[l], row(s_pool, l), conv_w2[l], row(conv_b, l),
                            row(conv_ln_g, l), row(conv_ln_b, l), w_pw_b[l], nc, (ctx_len, t_len))
        h = _tail(h, mod, row(g_post_mix, l), row(g_pre_ffn, l), row(g_post_ffn, l), ya, yf, yb, pn,
                  yc, yd, w_out_b[l], wg_b[l], wu_b[l], wd_b[l], nc, latent_only=l == depth - 1)
    return h
```

```python
import functools
import math

import jax
import jax.numpy as jnp
from jax import lax
from jax.experimental import pallas as pl
from jax.experimental.pallas import tpu as pltpu

F32 = jnp.float32
BF16 = jnp.bfloat16

GRID_W = 64
ROPE_BASE = 10000.0
A_HEADS = 4
B_HEADS = 4
POOL_WINDOWS = (2, 4, 8, 16)
CONV_K = 31
LOG2_E = math.log2(math.e)

LANES = 128
SUBLANES = 8
MXU_TILE = 256
ROW_TILE = 256
HALO = 16
VMEM_LIMIT = 56 * 1024 * 1024
TAIL_SUBBLOCKS = 2
ATTN_KEY_TILE = 4096
ATTN_SKEW = 3
ATTN_HEADROOM = 8.0
ATTN_SUM_ROWS = 16


def _cparams(sem):
    return pltpu.CompilerParams(dimension_semantics=sem, vmem_limit_bytes=VMEM_LIMIT)


def _rms(x, eps):
    return x * lax.rsqrt(jnp.mean(x * x, axis=-1, keepdims=True) + eps)


def _dot(a, b):
    return jnp.dot(a, b, preferred_element_type=F32)


def _mod_kernel(c_ref, w_ref, b_ref, o_ref):
    a = jax.nn.silu(c_ref[...])
    o_ref[...] = jnp.dot(a, w_ref[...], preferred_element_type=F32,
                         precision=lax.Precision.HIGHEST) + b_ref[...]


def _modulation(cond, w_ada, b_ada):
    n_layers, d, n6 = w_ada.shape
    tn = 2048
    return pl.pallas_call(
        _mod_kernel,
        grid=(n_layers, n6 // tn),
        in_specs=[pl.BlockSpec((8, d), lambda l, j: (0, 0)),
                  pl.BlockSpec((None, d, tn), lambda l, j: (l, 0, j)),
                  pl.BlockSpec((None, 1, tn), lambda l, j: (l, 0, j))],
        out_specs=pl.BlockSpec((None, 8, tn), lambda l, j: (l, 0, j)),
        out_shape=jax.ShapeDtypeStruct((n_layers, 8, n6), F32),
        compiler_params=_cparams(("parallel", "parallel")),
        name="adaln_mod",
    )(cond, w_ada, b_ada.reshape(n_layers, 1, n6))


def _inproj_kernel(a_scale, b_scale, h_ref, mod_ref, g_ref, wn_ref, wt_ref, tn_ref, tt_ref,
                   pn_ref, pt_ref):
    mod = mod_ref[...]
    u = _rms(h_ref[...], 1e-6) * g_ref[...]
    ub = (u * (1.0 + mod[1:2]) + mod[0:1]).astype(BF16)

    def nat(c0, c1):
        return _dot(ub, wn_ref[:, c0:c1])

    def tr(r0, r1):
        return lax.dot_general(wt_ref[r0:r1, :], ub, (((1,), (1,)), ((), ())),
                               preferred_element_type=F32)

    pn_ref[:, 0:512] = nat(0, 512).astype(BF16)
    ka = nat(512, 768)
    x1, x2 = ka[:, :LANES], ka[:, LANES:]
    cs, sn = tn_ref[:, 0:128], tn_ref[:, 128:256]
    pn_ref[:, 512:640] = (x1 * cs - x2 * sn).astype(BF16)
    pn_ref[:, 640:768] = (x1 * sn + x2 * cs).astype(BF16)
    qb = nat(768, 1024)
    x1, x2 = qb[:, :LANES], qb[:, LANES:]
    cs, sn = tn_ref[:, 256:384], tn_ref[:, 384:512]
    pn_ref[:, 768:896] = (x1 * cs - x2 * sn).astype(BF16)
    pn_ref[:, 896:1024] = (x1 * sn + x2 * cs).astype(BF16)
    pn_ref[:, 1024:2048] = nat(1024, 2048).astype(BF16)

    qa = tr(0, 256) * a_scale
    x1, x2 = qa[:LANES], qa[LANES:]
    cs, sn = tt_ref[0:128, :], tt_ref[128:256, :]
    pt_ref[0:128, :] = (x1 * cs - x2 * sn).astype(BF16)
    pt_ref[128:256, :] = (x1 * sn + x2 * cs).astype(BF16)
    kb = tr(256, 512) * b_scale
    x1, x2 = kb[:LANES], kb[LANES:]
    cs, sn = tt_ref[256:384, :], tt_ref[384:512, :]
    pt_ref[256:384, :] = (x1 * cs - x2 * sn).astype(BF16)
    pt_ref[384:512, :] = (x1 * sn + x2 * cs).astype(BF16)
    pt_ref[512:768, :] = tr(512, 768).astype(BF16)


def _inproj(h, mod, g_pre, w_nat, w_tr, tab_n, tab_t, nc, a_scale, b_scale):
    bsz, s, d = h.shape
    nt = s // ROW_TILE
    nn, ntr = w_nat.shape[1], w_tr.shape[0]
    msel = lambda b, i: (jnp.where(i < nc, bsz, b), 0, 0)
    return pl.pallas_call(
        functools.partial(_inproj_kernel, a_scale, b_scale),
        grid=(bsz, nt),
        in_specs=[pl.BlockSpec((None, ROW_TILE, d), lambda b, i: (b, i, 0)),
                  pl.BlockSpec((None, 6, d), msel),
                  pl.BlockSpec((1, d), lambda b, i: (0, 0)),
                  pl.BlockSpec((d, nn), lambda b, i: (0, 0)),
                  pl.BlockSpec((ntr, d), lambda b, i: (0, 0)),
                  pl.BlockSpec((ROW_TILE, 4 * LANES), lambda b, i: (i, 0)),
                  pl.BlockSpec((4 * LANES, ROW_TILE), lambda b, i: (0, i))],
        out_specs=[pl.BlockSpec((None, ROW_TILE, nn), lambda b, i: (b, i, 0)),
                   pl.BlockSpec((None, ntr, ROW_TILE), lambda b, i: (b, 0, i))],
        out_shape=[jax.ShapeDtypeStruct((bsz, s, nn), BF16),
                   jax.ShapeDtypeStruct((bsz, ntr, s), BF16)],
        compiler_params=_cparams(("parallel", "parallel")),
        name="in_proj",
    )(h, mod, g_pre, w_nat, w_tr, tab_n, tab_t)


def _attn_kernel(nc, n_ksteps, lam_init, qt_ref, kc_ref, vtc_ref, kl_ref, vtl_ref, lamv_ref, g_ref,
                 o_ref, qz_ref, m_ref, acc_ref, tmax_ref, pv_ref):
    qi = pl.program_id(1)
    ki = pl.program_id(2)
    n_groups = 2 * A_HEADS
    dv = vtc_ref.shape[0] // A_HEADS

    def attend(k_ref, vt_ref):
        k = k_ref[...]
        ones = jnp.ones((ATTN_SUM_ROWS, k.shape[0]), BF16)
        s_q = [_dot(k, qz_ref[g]) for g in range(ATTN_SKEW)]
        for g in range(n_groups):
            hd = g // 2
            s = s_q.pop(0)
            if g + ATTN_SKEW < n_groups:
                s_q.append(_dot(k, qz_ref[g + ATTN_SKEW]))
            m_old = m_ref[g]
            m_new = jnp.maximum(m_old, jnp.max(s, axis=0, keepdims=True))
            p = jnp.exp2((s - m_new).astype(BF16))
            v_ext = jnp.concatenate([vt_ref[hd * dv:(hd + 1) * dv, :], ones], axis=0)
            acc_ref[g] = jnp.exp2(m_old - m_new) * acc_ref[g] + _dot(v_ext, p)
            m_ref[g] = m_new

    def attend_lagged(k_ref, vt_ref):
        k = k_ref[...]
        ones = jnp.ones((ATTN_SUM_ROWS, k.shape[0]), BF16)
        s_q = [_dot(k, qz_ref[g]) for g in range(ATTN_SKEW)]
        excess = None
        for g in range(n_groups):
            hd = g // 2
            s = s_q.pop(0)
            if g + ATTN_SKEW < n_groups:
                s_q.append(_dot(k, qz_ref[g + ATTN_SKEW]))
            m_fix = m_ref[g]
            p = jnp.exp2((s - m_fix).astype(BF16))
            t_max = jnp.max(s, axis=0, keepdims=True)
            tmax_ref[g] = t_max
            excess = t_max - m_fix if excess is None else jnp.maximum(excess, t_max - m_fix)
            v_ext = jnp.concatenate([vt_ref[hd * dv:(hd + 1) * dv, :], ones], axis=0)
            pv_ref[g] = _dot(v_ext, p)
        return jnp.max(excess)

    def commit_lagged():
        for g in range(n_groups):
            m_old = m_ref[g]
            m_new = jnp.maximum(m_old, tmax_ref[g])
            acc_ref[g] = (acc_ref[g] + pv_ref[g]) * jnp.exp2(m_old - m_new)
            m_ref[g] = m_new

    @pl.when(ki == 0)
    def _first():
        qt = qt_ref[...]
        rg = (lax.broadcasted_iota(jnp.int32, qt.shape, 0) & (LANES - 1)) >> 4
        for g in range(n_groups):
            qz_ref[g] = jnp.where(rg == g, qt, jnp.zeros_like(qt))
        m_ref[...] = jnp.full(m_ref.shape, -1e30, F32)
        acc_ref[...] = jnp.zeros(acc_ref.shape, F32)
        attend(kc_ref, vtc_ref)

    @pl.when((ki > 0) & (qi >= nc))
    def _latent():
        within = attend_lagged(kl_ref, vtl_ref) <= ATTN_HEADROOM

        @pl.when(within)
        def _commit():
            commit_lagged()

        @pl.when(jnp.logical_not(within))
        def _redo():
            attend(kl_ref, vtl_ref)

    @pl.when(ki == n_ksteps - 1)
    def _fin():
        lv = lamv_ref[...]
        a1 = jnp.sum(lv[0:1] * lv[1:2], axis=-1, keepdims=True)
        a2 = jnp.sum(lv[2:3] * lv[3:4], axis=-1, keepdims=True)
        lam = jnp.exp(a1) - jnp.exp(a2) + lam_init
        outs = []
        for hd in range(A_HEADS):
            a_pos, a_neg = acc_ref[2 * hd], acc_ref[2 * hd + 1]
            o = a_pos[:dv] / a_pos[dv:dv + 1] - lam * (a_neg[:dv] / a_neg[dv:dv + 1])
            ms = jnp.mean(o * o, axis=0, keepdims=True)
            outs.append(o * lax.rsqrt(ms + 1e-5) * g_ref[...] * (1.0 - lam_init))
        o_ref[...] = jnp.concatenate(outs, axis=0).T.astype(o_ref.dtype)


def _attention(pn, pt, ctx_len, lamv, g_col, nc, lam_init):
    bsz, s, _ = pn.shape
    t_len = s - ctx_len
    nq = s // ROW_TILE
    tk = min(ATTN_KEY_TILE, t_len)
    assert t_len % tk == 0
    n_ksteps = 1 + t_len // tk
    w = MXU_TILE
    dv_ext = w // A_HEADS + ATTN_SUM_ROWS
    kl = lambda qi, ki: pl.multiple_of(ctx_len + jnp.where(qi < nc, 0, jnp.maximum(ki - 1, 0)) * tk, ROW_TILE)
    return pl.pallas_call(
        functools.partial(_attn_kernel, nc, n_ksteps, lam_init),
        grid=(bsz, nq, n_ksteps),
        in_specs=[pl.BlockSpec((None, w, ROW_TILE), lambda b, qi, ki: (b, 0, qi)),
                  pl.BlockSpec((None, ctx_len, w), lambda b, qi, ki: (b, 0, 2)),
                  pl.BlockSpec((None, w, ctx_len), lambda b, qi, ki: (b, 2, 0)),
                  pl.BlockSpec((None, pl.Element(tk), pl.Element(w)), lambda b, qi, ki: (b, kl(qi, ki), 2 * w)),
                  pl.BlockSpec((None, pl.Element(w), pl.Element(tk)), lambda b, qi, ki: (b, 2 * w, kl(qi, ki))),
                  pl.BlockSpec(lamv.shape, lambda b, qi, ki: (0, 0)),
                  pl.BlockSpec(g_col.shape, lambda b, qi, ki: (0, 0))],
        out_specs=pl.BlockSpec((None, ROW_TILE, w), lambda b, qi, ki: (b, qi, 0)),
        out_shape=jax.ShapeDtypeStruct((bsz, s, w), BF16),
        scratch_shapes=[pltpu.VMEM((2 * A_HEADS, w, ROW_TILE), BF16),
                        pltpu.VMEM((2 * A_HEADS, 1, ROW_TILE), F32),
                        pltpu.VMEM((2 * A_HEADS, dv_ext, ROW_TILE), F32),
                        pltpu.VMEM((2 * A_HEADS, 1, ROW_TILE), F32),
                        pltpu.VMEM((2 * A_HEADS, dv_ext, ROW_TILE), F32)],
        compiler_params=_cparams(("parallel", "parallel", "arbitrary")),
        name="diff_attn",
    )(pt, pn, pt, pn, pt, lamv, g_col)


def _group_norm(y, gmat, eps):
    yh = y.astype(BF16)
    yl = (y - yh.astype(F32)).astype(BF16)
    d = y - (_dot(yh, gmat) + _dot(yl, gmat))
    return d * lax.rsqrt(_dot((d * d).astype(BF16), gmat) + eps)


def _ret_kernel(qf_ref, ktf_ref, vf_ref, qb_ref, ktb_ref, vb_ref, dm_ref, wend_ref, cross_ref,
                decs_ref, bd_ref, gm_ref, yf_ref, yb_ref, s_ref):
    i = pl.program_id(1)

    @pl.when(i == 0)
    def _init():
        s_ref[...] = jnp.zeros(s_ref.shape, F32)

    def one_dir(dr, q_ref, kt_ref, v_ref, y_ref):
        q = q_ref[...]
        kt = kt_ref[...]
        v = v_ref[...]
        rh = (lax.broadcasted_iota(jnp.int32, kt.shape, 0) & (LANES - 1)) >> 5
        ch = lax.broadcasted_iota(jnp.int32, (q.shape[0], v.shape[1]), 1) >> 6
        y = jnp.zeros((q.shape[0], v.shape[1]), F32)
        for hd in range(B_HEADS):
            ktz = jnp.where(rh == hd, kt, jnp.zeros_like(kt))
            p = (_dot(q, ktz) * dm_ref[dr, hd]).astype(BF16)
            y = jnp.where(ch == hd, _dot(p, v), y)
        s_old = s_ref[dr]
        y = y + _dot(q, s_old.astype(BF16)) * cross_ref[dr]
        kw = (kt.astype(F32) * wend_ref[dr]).astype(BF16)
        s_ref[dr] = decs_ref[...] * s_old + bd_ref[...] * _dot(kw, v)
        y_ref[...] = _group_norm(y, gm_ref[...], 1e-6).astype(y_ref.dtype)

    one_dir(0, qf_ref, ktf_ref, vf_ref, yf_ref)
    one_dir(1, qb_ref, ktb_ref, vb_ref, yb_ref)


def _retention(pn, pt, tabs, nc):
    bsz, s, _ = pn.shape
    c = ROW_TILE
    nt = s // c
    w = MXU_TILE
    dmask, wend, cross, decs, bdm, gmat = tabs
    cb = lambda i: jnp.where(i < nc, nc - 1 - i, nt - 1 - (i - nc))
    const = lambda a: pl.BlockSpec(a.shape, lambda b, i: (0,) * a.ndim)
    return pl.pallas_call(
        _ret_kernel,
        grid=(bsz, nt),
        in_specs=[pl.BlockSpec((None, c, w), lambda b, i: (b, i, 3)),
                  pl.BlockSpec((None, w, c), lambda b, i: (b, 1, i)),
                  pl.BlockSpec((None, c, w), lambda b, i: (b, i, 4)),
                  pl.BlockSpec((None, c, w), lambda b, i: (b, cb(i), 3)),
                  pl.BlockSpec((None, w, c), lambda b, i: (b, 1, cb(i))),
                  pl.BlockSpec((None, c, w), lambda b, i: (b, cb(i), 4)),
                  const(dmask), const(wend), const(cross), const(decs), const(bdm), const(gmat)],
        out_specs=[pl.BlockSpec((None, c, w), lambda b, i: (b, i, 0)),
                   pl.BlockSpec((None, c, w), lambda b, i: (b, cb(i), 0))],
        out_shape=[jax.ShapeDtypeStruct((bsz, s, w), BF16),
                   jax.ShapeDtypeStruct((bsz, s, w), BF16)],
        scratch_shapes=[pltpu.VMEM((2, w, w), F32)],
        compiler_params=_cparams(("parallel", "arbitrary")),
        name="retention",
    )(pn, pt, pn, pn, pt, pn, dmask, wend, cross, decs, bdm, gmat)


def _retention_tables(c):
    lg = jnp.asarray([math.log(1.0 - 2.0 ** (-5 - h)) for h in range(B_HEADS)], F32)
    j = jnp.arange(c, dtype=F32)
    dist = j[:, None] - j[None, :]
    df = jnp.where(dist >= 0, jnp.exp(lg[:, None, None] * jnp.maximum(dist, 0.0)), 0.0)
    dmask = jnp.stack([df, jnp.swapaxes(df, 1, 2)])
    row_head = (jnp.arange(MXU_TILE) % LANES) // 32
    col_head = jnp.arange(MXU_TILE) // 64
    wend_f = jnp.exp(lg[row_head][:, None] * (c - 1.0 - j)[None, :])
    wend_b = jnp.exp(lg[row_head][:, None] * j[None, :])
    cross_f = jnp.exp(lg[col_head][None, :] * (j + 1.0)[:, None])
    cross_b = jnp.exp(lg[col_head][None, :] * (c - j)[:, None])
    bdm = (row_head[:, None] == col_head[None, :]).astype(F32)
    decs = bdm * jnp.exp(lg * c)[col_head][None, :]
    gmat = ((col_head[:, None] == col_head[None, :]).astype(F32) / 64.0).astype(BF16)
    return (dmask, jnp.stack([wend_f, wend_b]), jnp.stack([cross_f, cross_b]), decs, bdm, gmat)


def _cd_kernel(nc, nt, seg_lens, glu_ref, glu_p, glu_n, pool_ref, pool_p, pool_n, wpool_ref,
               spool_ref, cw_ref, cb_ref, lng_ref, lnb_ref, wpw_ref, yc_ref, yd_ref, ext_ref,
               shift_ref):
    i = pl.program_id(1)
    r = pool_ref.shape[0]
    w = pool_ref.shape[1]
    prev_ok = ((i != 0) & (i != nc)).astype(F32)
    next_ok = ((i != nc - 1) & (i != nt - 1)).astype(F32)

    def fill_shifted():
        n = shift_ref.shape[1]
        for b in range(1, SUBLANES):
            shift_ref[b - 1] = ext_ref[b:b + n, :]

    def rows_at(o, n):
        a, b = divmod(o, SUBLANES)
        if b == 0:
            return ext_ref[o:o + n, :]
        return shift_ref[b - 1, SUBLANES * a:SUBLANES * a + n, :]

    x = pool_ref[...].astype(F32)
    ext_ref[0:HALO, :] = pool_p[...].astype(F32) * prev_ok
    ext_ref[HALO:HALO + r, :] = x
    ext_ref[HALO + r:, :] = pool_n[...].astype(F32) * next_ok
    fill_shifted()

    def sh(k):
        return rows_at(HALO + k, r)

    w2 = sh(-1) + x
    w4 = w2 + sh(-2) + sh(1)
    w8 = w4 + sh(-4) + sh(-3) + sh(2) + sh(3)
    w16 = w8 + sh(-8) + sh(-7) + sh(-6) + sh(-5) + sh(4) + sh(5) + sh(6) + sh(7)
    grp = lax.broadcasted_iota(jnp.int32, (r, w), 1) >> 6
    wsum = jnp.where(grp == 0, w2, jnp.where(grp == 1, w4, jnp.where(grp == 2, w8, w16)))
    half = jnp.left_shift(1, grp)
    in_ctx = i < nc
    seg_len = jnp.where(in_ctx, seg_lens[0], seg_lens[1])
    pos = lax.broadcasted_iota(jnp.int32, (r, w), 0) + (i - jnp.where(in_ctx, 0, nc)) * r
    cnt = jnp.minimum(pos + half, seg_len) - jnp.maximum(pos - half, 0)
    dlt = wsum / cnt.astype(F32) - x
    yc_ref[...] = (_dot(dlt.astype(BF16), wpool_ref[...]) * spool_ref[...]).astype(yc_ref.dtype)

    def glu(ref):
        v = ref[...].astype(F32)
        return v[:, :w] * jax.nn.sigmoid(v[:, w:])

    ext_ref[0:HALO, :] = glu(glu_p) * prev_ok
    ext_ref[HALO:HALO + r, :] = glu(glu_ref)
    ext_ref[HALO + r:, :] = glu(glu_n) * next_ok
    fill_shifted()
    sub = 64
    for r0 in range(0, r, sub):
        acc = jnp.zeros((sub, w), F32) + cb_ref[...]
        for k in range(CONV_K):
            acc = acc + rows_at(HALO - CONV_K // 2 + k + r0, sub) * cw_ref[k:k + 1, :]
        mu = jnp.mean(acc, axis=-1, keepdims=True)
        d = acc - mu
        var = jnp.mean(d * d, axis=-1, keepdims=True)
        hn = d * lax.rsqrt(var + 1e-5) * lng_ref[...] + lnb_ref[...]
        yd_ref[r0:r0 + sub, :] = _dot(jax.nn.silu(hn).astype(BF16), wpw_ref[...]).astype(yd_ref.dtype)


def _pool_conv(pn, wpool_bd, spool, conv_w, conv_b, ln_g, ln_b, w_pw, nc, seg_lens):
    bsz, s, _ = pn.shape
    r = ROW_TILE
    nt = s // r
    w = MXU_TILE
    hb = r // HALO
    nhb = s // HALO
    prev = lambda i: jnp.maximum(i * hb - 1, 0)
    nxt = lambda i: jnp.minimum((i + 1) * hb, nhb - 1)
    const = lambda a: pl.BlockSpec(a.shape, lambda b, i: (0,) * a.ndim)
    return pl.pallas_call(
        functools.partial(_cd_kernel, nc, nt, seg_lens),
        grid=(bsz, nt),
        in_specs=[pl.BlockSpec((None, r, 2 * w), lambda b, i: (b, i, 0)),
                  pl.BlockSpec((None, HALO, 2 * w), lambda b, i: (b, prev(i), 0)),
                  pl.BlockSpec((None, HALO, 2 * w), lambda b, i: (b, nxt(i), 0)),
                  pl.BlockSpec((None, r, w), lambda b, i: (b, i, 7)),
                  pl.BlockSpec((None, HALO, w), lambda b, i: (b, prev(i), 7)),
                  pl.BlockSpec((None, HALO, w), lambda b, i: (b, nxt(i), 7)),
                  const(wpool_bd), const(spool), const(conv_w), const(conv_b), const(ln_g),
                  const(ln_b), const(w_pw)],
        out_specs=[pl.BlockSpec((None, r, w), lambda b, i: (b, i, 0)),
                   pl.BlockSpec((None, r, w), lambda b, i: (b, i, 0))],
        out_shape=[jax.ShapeDtypeStruct((bsz, s, w), BF16),
                   jax.ShapeDtypeStruct((bsz, s, w), BF16)],
        scratch_shapes=[pltpu.VMEM((r + 2 * HALO, w), F32),
                        pltpu.VMEM((SUBLANES - 1, r + 2 * HALO - SUBLANES, w), F32)],
        compiler_params=_cparams(("parallel", "parallel")),
        name="pool_conv",
    )(pn, pn, pn, pn, pn, pn, wpool_bd, spool, conv_w, conv_b, ln_g, ln_b, w_pw)


def _tail_kernel(h_ref, mod_ref, gpm_ref, gprf_ref, gpof_ref, ya_ref, yf_ref, yb_ref, gf_ref, gb_ref,
                 yc_ref, yd_ref, wo_ref, wg_ref, wu_ref, wd_ref, o_ref):
    rb = h_ref.shape[0] // TAIL_SUBBLOCKS
    rows = [pl.ds(j * rb, rb) for j in range(TAIL_SUBBLOCKS)]

    def mix_in(rs):
        yb = (jax.nn.silu(gf_ref[rs, :].astype(F32)) * yf_ref[rs, :].astype(F32)
              + jax.nn.silu(gb_ref[rs, :].astype(F32)) * yb_ref[rs, :].astype(F32))
        ycat = jnp.concatenate([ya_ref[rs, :], yb.astype(BF16), yc_ref[rs, :], yd_ref[rs, :]], axis=-1)
        return _dot(ycat, wo_ref[...])

    ys = [mix_in(rs) for rs in rows]
    xs, us = [], []
    for rs, y in zip(rows, ys):
        x = h_ref[rs, :] + mod_ref[2:3, :] * (_rms(y, 1e-6) * gpm_ref[...])
        u = _rms(x, 1e-6) * gprf_ref[...]
        xs.append(x)
        us.append((u * (1.0 + mod_ref[4:5, :]) + mod_ref[3:4, :]).astype(BF16))
    acts = [(jax.nn.silu(_dot(ub, wg_ref[...])) * _dot(ub, wu_ref[...])).astype(BF16) for ub in us]
    fs = [_dot(a, wd_ref[...]) for a in acts]
    for rs, x, f in zip(rows, xs, fs):
        o_ref[rs, :] = x + mod_ref[5:6, :] * (_rms(f, 1e-6) * gpof_ref[...])


def _tail(h, mod, g_post_mix, g_pre_ffn, g_post_ffn, ya, yf, yb, pn, yc, yd, w_out, wg, wu, wd, nc,
          latent_only):
    bsz, s, d = h.shape
    r = ROW_TILE
    nt = s // r
    out_rows = s - nc * r if latent_only else s
    out_tile = (lambda b, i: (b, jnp.maximum(i - nc, 0), 0)) if latent_only else (lambda b, i: (b, i, 0))
    w = MXU_TILE
    msel = lambda b, i: (jnp.where(i < nc, bsz, b), 0, 0)
    tile = lambda col: pl.BlockSpec((None, r, w), lambda b, i: (b, i, col))
    vec = pl.BlockSpec((1, d), lambda b, i: (0, 0))
    resident = lambda a: pl.BlockSpec(a.shape, lambda b, i: (0, 0), pipeline_mode=pl.Buffered(1))
    return pl.pallas_call(
        _tail_kernel,
        grid=(bsz, nt),
        in_specs=[pl.BlockSpec((None, r, d), lambda b, i: (b, i, 0)),
                  pl.BlockSpec((None, 6, d), msel),
                  vec, vec, vec,
                  tile(0), tile(0), tile(0), tile(5), tile(6), tile(0), tile(0),
                  resident(w_out), resident(wg), resident(wu), resident(wd)],
        out_specs=pl.BlockSpec((None, r, d), out_tile),
        out_shape=jax.ShapeDtypeStruct((bsz, out_rows, d), F32),
        compiler_params=_cparams(("parallel", "arbitrary")),
        name="out_ffn",
    )(h, mod, g_post_mix, g_pre_ffn, g_post_ffn, ya, yf, yb, pn, pn, yc, yd, w_out, wg, wu, wd)


def _rope_tables(t_len, ctx_len, dqk_a, dk_b):
    rows = t_len // GRID_W
    row = jnp.repeat(jnp.arange(rows, dtype=F32), GRID_W)
    col = jnp.tile(jnp.arange(GRID_W, dtype=F32), rows)

    def cs(d, reps):
        n_freq = d // 4
        inv = ROPE_BASE ** (-jnp.arange(n_freq, dtype=F32) / n_freq)
        ang = jnp.concatenate([row[:, None] * inv, col[:, None] * inv], axis=-1)
        cos = jnp.concatenate([jnp.ones((ctx_len, d // 2), F32), jnp.cos(ang)], axis=0)
        sin = jnp.concatenate([jnp.zeros((ctx_len, d // 2), F32), jnp.sin(ang)], axis=0)
        return jnp.tile(cos, (1, reps)), jnp.tile(sin, (1, reps))

    ca, sa = cs(dqk_a, LANES // (dqk_a // 2))
    cb, sb = cs(dk_b, LANES // (dk_b // 2))
    tab_n = jnp.concatenate([ca, sa, cb, sb], axis=1)
    return tab_n, tab_n.T


def _rotary_perm(n_groups, d):
    half, g, i = jnp.meshgrid(jnp.arange(2), jnp.arange(n_groups), jnp.arange(d // 2), indexing="ij")
    return (g * d + half * (d // 2) + i).reshape(-1)


def kernel(x, c, ctx, c_ctx, w_ada, b_ada, g_pre_mix, g_post_mix, g_pre_ffn, g_post_ffn, w_in, w_out, lam_q1, lam_k1, lam_q2, lam_k2, g_subln, w_pool, s_pool, conv_w, conv_b, conv_ln_g, conv_ln_b, w_conv_out, w_ffn_gate, w_ffn_up, w_ffn_down):
    bsz, t_len, d = x.shape
    ctx_len = ctx.shape[1]
    depth = w_in.shape[0]
    assert d == 8 * A_HEADS * 32 and t_len % ROW_TILE == 0 and ctx_len % ROW_TILE == 0
    assert t_len % GRID_W == 0 and bsz < 8
    qw = d // 4
    dqk_a = qw // (2 * A_HEADS)
    dk_b = qw // B_HEADS
    nc = ctx_len // ROW_TILE

    h = jnp.concatenate([ctx, x], axis=1)

    cond = jnp.zeros((8, d), F32).at[:bsz].set(c).at[bsz].set(c_ctx)
    mods = _modulation(cond, w_ada, b_ada).reshape(depth, 8, 6, d)

    pa = _rotary_perm(2 * A_HEADS, dqk_a)
    pb = _rotary_perm(B_HEADS, dk_b)
    col = lambda k: w_in[:, :, k * qw:(k + 1) * qw]
    k_a, v_a, k_b, v_b, q_a, q_b, g_f, g_b, pool = (col(k) for k in range(9))
    glu = w_in[:, :, 9 * qw:]
    w_nat = jnp.concatenate([glu, k_a[:, :, pa], q_b[:, :, pb], v_b, g_f, g_b, pool], axis=-1).astype(BF16)
    w_tr = jnp.swapaxes(jnp.concatenate([q_a[:, :, pa], k_b[:, :, pb], v_a], axis=-1), 1, 2).astype(BF16)
    w_out_b = w_out.astype(BF16)
    wg_b, wu_b, wd_b = w_ffn_gate.astype(BF16), w_ffn_up.astype(BF16), w_ffn_down.astype(BF16)
    w_pw_b = w_conv_out.astype(BF16)
    eye = jnp.eye(len(POOL_WINDOWS), dtype=F32)
    wpool_bd = jnp.einsum("lgce,gh->lgche", w_pool, eye).reshape(depth, qw, qw).astype(BF16)
    conv_w2 = jnp.pad(conv_w.reshape(depth, CONV_K, qw), ((0, 0), (0, 1), (0, 0)))
    lamv = jnp.stack([lam_q1, lam_k1, lam_q2, lam_k2], axis=1)
    g_col = jnp.tile(g_subln, (1, 1))[:, :, None]

    tab_n, tab_t = _rope_tables(t_len, ctx_len, dqk_a, dk_b)
    ret_tabs = _retention_tables(ROW_TILE)
    row = lambda a, l: a[l][None, :]

    for l in range(depth):
        lam_init = 0.8 - 0.6 * math.exp(-0.3 * l)
        mod = mods[l]
        pn, pt = _inproj(h, mod, row(g_pre_mix, l), w_nat[l], w_tr[l], tab_n, tab_t, nc,
                         dqk_a ** -0.5 * LOG2_E, dk_b ** -0.5)
        ya = _attention(pn, pt, ctx_len, lamv[l], g_col[l], nc, lam_init)
        yf, yb = _retention(pn, pt, ret_tabs, nc)
        yc, yd = _pool_conv(pn, wpool_bd[l], row(s_pool, l), conv_w2[l], row(conv_b, l),
                            row(conv_ln_g, l), row(conv_ln_b, l), w_pw_b[l], nc, (ctx_len, t_len))
        h = _tail(h, mod, row(g_post_mix, l), row(g_pre_ffn, l), row(g_post_ffn, l), ya, yf, yb, pn,
                  yc, yd, w_out_b[l], wg_b[l], wu_b[l], wd_b[l], nc, latent_only=l == depth - 1)
    return h
```

```python
import functools
import math

import jax
import jax.numpy as jnp
from jax import lax
from jax.experimental import pallas as pl
from jax.experimental.pallas import tpu as pltpu

F32 = jnp.float32
BF16 = jnp.bfloat16

GRID_W = 64
ROPE_BASE = 10000.0
A_HEADS = 4
B_HEADS = 4
POOL_WINDOWS = (2, 4, 8, 16)
CONV_K = 31
LOG2_E = math.log2(math.e)

LANES = 128
SUBLANES = 8
MXU_TILE = 256
ROW_TILE = 256
HALO = 16
VMEM_LIMIT = 56 * 1024 * 1024
TAIL_SUBBLOCKS = 2
ATTN_KEY_TILE = 4096
ATTN_SKEW = 3
ATTN_HEADROOM = 8.0
ATTN_SUM_ROWS = 16
CONV_ROWS = 64


def _cparams(sem):
    return pltpu.CompilerParams(dimension_semantics=sem, vmem_limit_bytes=VMEM_LIMIT)


def _rms(x, eps):
    return x * lax.rsqrt(jnp.mean(x * x, axis=-1, keepdims=True) + eps)


def _dot(a, b):
    return jnp.dot(a, b, preferred_element_type=F32)


def _mod_kernel(c_ref, w_ref, b_ref, o_ref):
    a = jax.nn.silu(c_ref[...])
    o_ref[...] = jnp.dot(a, w_ref[...], preferred_element_type=F32,
                         precision=lax.Precision.HIGHEST) + b_ref[...]


def _modulation(cond, w_ada, b_ada):
    n_layers, d, n6 = w_ada.shape
    tn = 2048
    return pl.pallas_call(
        _mod_kernel,
        grid=(n_layers, n6 // tn),
        in_specs=[pl.BlockSpec((8, d), lambda l, j: (0, 0)),
                  pl.BlockSpec((None, d, tn), lambda l, j: (l, 0, j)),
                  pl.BlockSpec((None, 1, tn), lambda l, j: (l, 0, j))],
        out_specs=pl.BlockSpec((None, 8, tn), lambda l, j: (l, 0, j)),
        out_shape=jax.ShapeDtypeStruct((n_layers, 8, n6), F32),
        compiler_params=_cparams(("parallel", "parallel")),
        name="adaln_mod",
    )(cond, w_ada, b_ada.reshape(n_layers, 1, n6))


def _inproj_kernel(a_scale, b_scale, h_ref, mod_ref, g_ref, wn_ref, wt_ref, tn_ref, tt_ref,
                   pn_ref, pt_ref):
    mod = mod_ref[...]
    u = _rms(h_ref[...], 1e-6) * g_ref[...]
    ub = (u * (1.0 + mod[1:2]) + mod[0:1]).astype(BF16)

    def nat(c0, c1):
        return _dot(ub, wn_ref[:, c0:c1])

    def tr(r0, r1):
        return lax.dot_general(wt_ref[r0:r1, :], ub, (((1,), (1,)), ((), ())),
                               preferred_element_type=F32)

    pn_ref[:, 0:512] = nat(0, 512).astype(BF16)
    ka = nat(512, 768)
    x1, x2 = ka[:, :LANES], ka[:, LANES:]
    cs, sn = tn_ref[:, 0:128], tn_ref[:, 128:256]
    pn_ref[:, 512:640] = (x1 * cs - x2 * sn).astype(BF16)
    pn_ref[:, 640:768] = (x1 * sn + x2 * cs).astype(BF16)
    qb = nat(768, 1024)
    x1, x2 = qb[:, :LANES], qb[:, LANES:]
    cs, sn = tn_ref[:, 256:384], tn_ref[:, 384:512]
    pn_ref[:, 768:896] = (x1 * cs - x2 * sn).astype(BF16)
    pn_ref[:, 896:1024] = (x1 * sn + x2 * cs).astype(BF16)
    pn_ref[:, 1024:2048] = nat(1024, 2048).astype(BF16)

    qa = tr(0, 256) * a_scale
    x1, x2 = qa[:LANES], qa[LANES:]
    cs, sn = tt_ref[0:128, :], tt_ref[128:256, :]
    pt_ref[0:128, :] = (x1 * cs - x2 * sn).astype(BF16)
    pt_ref[128:256, :] = (x1 * sn + x2 * cs).astype(BF16)
    kb = tr(256, 512) * b_scale
    x1, x2 = kb[:LANES], kb[LANES:]
    cs, sn = tt_ref[256:384, :], tt_ref[384:512, :]
    pt_ref[256:384, :] = (x1 * cs - x2 * sn).astype(BF16)
    pt_ref[384:512, :] = (x1 * sn + x2 * cs).astype(BF16)
    pt_ref[512:768, :] = tr(512, 768).astype(BF16)


def _inproj(h, mod, g_pre, w_nat, w_tr, tab_n, tab_t, nc, a_scale, b_scale):
    bsz, s, d = h.shape
    nt = s // ROW_TILE
    nn, ntr = w_nat.shape[1], w_tr.shape[0]
    msel = lambda b, i: (jnp.where(i < nc, bsz, b), 0, 0)
    return pl.pallas_call(
        functools.partial(_inproj_kernel, a_scale, b_scale),
        grid=(bsz, nt),
        in_specs=[pl.BlockSpec((None, ROW_TILE, d), lambda b, i: (b, i, 0)),
                  pl.BlockSpec((None, 6, d), msel),
                  pl.BlockSpec((1, d), lambda b, i: (0, 0)),
                  pl.BlockSpec((d, nn), lambda b, i: (0, 0)),
                  pl.BlockSpec((ntr, d), lambda b, i: (0, 0)),
                  pl.BlockSpec((ROW_TILE, 4 * LANES), lambda b, i: (i, 0)),
                  pl.BlockSpec((4 * LANES, ROW_TILE), lambda b, i: (0, i))],
        out_specs=[pl.BlockSpec((None, ROW_TILE, nn), lambda b, i: (b, i, 0)),
                   pl.BlockSpec((None, ntr, ROW_TILE), lambda b, i: (b, 0, i))],
        out_shape=[jax.ShapeDtypeStruct((bsz, s, nn), BF16),
                   jax.ShapeDtypeStruct((bsz, ntr, s), BF16)],
        compiler_params=_cparams(("parallel", "parallel")),
        name="in_proj",
    )(h, mod, g_pre, w_nat, w_tr, tab_n, tab_t)


def _attn_kernel(nc, n_ksteps, lam_init, qt_ref, kc_ref, vtc_ref, kl_ref, vtl_ref, lamv_ref, g_ref,
                 o_ref, qz_ref, m_ref, acc_ref, tmax_ref, pv_ref):
    qi = pl.program_id(1)
    ki = pl.program_id(2)
    n_groups = 2 * A_HEADS
    dv = vtc_ref.shape[0] // A_HEADS

    def attend(k_ref, vt_ref):
        k = k_ref[...]
        ones = jnp.ones((ATTN_SUM_ROWS, k.shape[0]), BF16)
        s_q = [_dot(k, qz_ref[g]) for g in range(ATTN_SKEW)]
        for g in range(n_groups):
            hd = g // 2
            s = s_q.pop(0)
            if g + ATTN_SKEW < n_groups:
                s_q.append(_dot(k, qz_ref[g + ATTN_SKEW]))
            m_old = m_ref[g]
            m_new = jnp.maximum(m_old, jnp.max(s, axis=0, keepdims=True))
            p = jnp.exp2((s - m_new).astype(BF16))
            v_ext = jnp.concatenate([vt_ref[hd * dv:(hd + 1) * dv, :], ones], axis=0)
            acc_ref[g] = jnp.exp2(m_old - m_new) * acc_ref[g] + _dot(v_ext, p)
            m_ref[g] = m_new

    def attend_lagged(k_ref, vt_ref):
        k = k_ref[...]
        ones = jnp.ones((ATTN_SUM_ROWS, k.shape[0]), BF16)
        s_q = [_dot(k, qz_ref[g]) for g in range(ATTN_SKEW)]
        excess = None
        for g in range(n_groups):
            hd = g // 2
            s = s_q.pop(0)
            if g + ATTN_SKEW < n_groups:
                s_q.append(_dot(k, qz_ref[g + ATTN_SKEW]))
            m_fix = m_ref[g]
            p = jnp.exp2((s - m_fix).astype(BF16))
            t_max = jnp.max(s, axis=0, keepdims=True)
            tmax_ref[g] = t_max
            excess = t_max - m_fix if excess is None else jnp.maximum(excess, t_max - m_fix)
            v_ext = jnp.concatenate([vt_ref[hd * dv:(hd + 1) * dv, :], ones], axis=0)
            pv_ref[g] = _dot(v_ext, p)
        return jnp.max(excess)

    def commit_lagged():
        for g in range(n_groups):
            m_old = m_ref[g]
            m_new = jnp.maximum(m_old, tmax_ref[g])
            acc_ref[g] = (acc_ref[g] + pv_ref[g]) * jnp.exp2(m_old - m_new)
            m_ref[g] = m_new

    @pl.when(ki == 0)
    def _first():
        qt = qt_ref[...]
        rg = (lax.broadcasted_iota(jnp.int32, qt.shape, 0) & (LANES - 1)) >> 4
        for g in range(n_groups):
            qz_ref[g] = jnp.where(rg == g, qt, jnp.zeros_like(qt))
        m_ref[...] = jnp.full(m_ref.shape, -1e30, F32)
        acc_ref[...] = jnp.zeros(acc_ref.shape, F32)
        attend(kc_ref, vtc_ref)

    @pl.when((ki > 0) & (qi >= nc))
    def _latent():
        within = attend_lagged(kl_ref, vtl_ref) <= ATTN_HEADROOM

        @pl.when(within)
        def _commit():
            commit_lagged()

        @pl.when(jnp.logical_not(within))
        def _redo():
            attend(kl_ref, vtl_ref)

    @pl.when(ki == n_ksteps - 1)
    def _fin():
        lv = lamv_ref[...]
        a1 = jnp.sum(lv[0:1] * lv[1:2], axis=-1, keepdims=True)
        a2 = jnp.sum(lv[2:3] * lv[3:4], axis=-1, keepdims=True)
        lam = jnp.exp(a1) - jnp.exp(a2) + lam_init
        outs = []
        for hd in range(A_HEADS):
            a_pos, a_neg = acc_ref[2 * hd], acc_ref[2 * hd + 1]
            o = a_pos[:dv] / a_pos[dv:dv + 1] - lam * (a_neg[:dv] / a_neg[dv:dv + 1])
            ms = jnp.mean(o * o, axis=0, keepdims=True)
            outs.append(o * lax.rsqrt(ms + 1e-5) * g_ref[...] * (1.0 - lam_init))
        o_ref[...] = jnp.concatenate(outs, axis=0).T.astype(o_ref.dtype)


def _attention(pn, pt, ctx_len, lamv, g_col, nc, lam_init):
    bsz, s, _ = pn.shape
    t_len = s - ctx_len
    nq = s // ROW_TILE
    tk = min(ATTN_KEY_TILE, t_len)
    assert t_len % tk == 0
    n_ksteps = 1 + t_len // tk
    w = MXU_TILE
    dv_ext = w // A_HEADS + ATTN_SUM_ROWS
    kl = lambda qi, ki: pl.multiple_of(ctx_len + jnp.where(qi < nc, 0, jnp.maximum(ki - 1, 0)) * tk, ROW_TILE)
    return pl.pallas_call(
        functools.partial(_attn_kernel, nc, n_ksteps, lam_init),
        grid=(bsz, nq, n_ksteps),
        in_specs=[pl.BlockSpec((None, w, ROW_TILE), lambda b, qi, ki: (b, 0, qi)),
                  pl.BlockSpec((None, ctx_len, w), lambda b, qi, ki: (b, 0, 2)),
                  pl.BlockSpec((None, w, ctx_len), lambda b, qi, ki: (b, 2, 0)),
                  pl.BlockSpec((None, pl.Element(tk), pl.Element(w)), lambda b, qi, ki: (b, kl(qi, ki), 2 * w)),
                  pl.BlockSpec((None, pl.Element(w), pl.Element(tk)), lambda b, qi, ki: (b, 2 * w, kl(qi, ki))),
                  pl.BlockSpec(lamv.shape, lambda b, qi, ki: (0, 0)),
                  pl.BlockSpec(g_col.shape, lambda b, qi, ki: (0, 0))],
        out_specs=pl.BlockSpec((None, ROW_TILE, w), lambda b, qi, ki: (b, qi, 0)),
        out_shape=jax.ShapeDtypeStruct((bsz, s, w), BF16),
        scratch_shapes=[pltpu.VMEM((2 * A_HEADS, w, ROW_TILE), BF16),
                        pltpu.VMEM((2 * A_HEADS, 1, ROW_TILE), F32),
                        pltpu.VMEM((2 * A_HEADS, dv_ext, ROW_TILE), F32),
                        pltpu.VMEM((2 * A_HEADS, 1, ROW_TILE), F32),
                        pltpu.VMEM((2 * A_HEADS, dv_ext, ROW_TILE), F32)],
        compiler_params=_cparams(("parallel", "parallel", "arbitrary")),
        name="diff_attn",
    )(pt, pn, pt, pn, pt, lamv, g_col)


def _group_norm(y, gmat, eps):
    yh = y.astype(BF16)
    yl = (y - yh.astype(F32)).astype(BF16)
    d = y - (_dot(yh, gmat) + _dot(yl, gmat))
    return d * lax.rsqrt(_dot((d * d).astype(BF16), gmat) + eps)


def _ret_direction(dr, q_ref, kt_ref, v_ref, y_ref, dm_ref, wend_ref, cross_ref, decs_ref, bd_ref,
                   gm_ref, s_ref):
    q = q_ref[...]
    kt = kt_ref[...]
    v = v_ref[...]
    rh = (lax.broadcasted_iota(jnp.int32, kt.shape, 0) & (LANES - 1)) >> 5
    ch = lax.broadcasted_iota(jnp.int32, (q.shape[0], v.shape[1]), 1) >> 6
    y = jnp.zeros((q.shape[0], v.shape[1]), F32)
    for hd in range(B_HEADS):
        ktz = jnp.where(rh == hd, kt, jnp.zeros_like(kt))
        p = (_dot(q, ktz) * dm_ref[dr, hd]).astype(BF16)
        y = jnp.where(ch == hd, _dot(p, v), y)
    s_old = s_ref[dr]
    y = y + _dot(q, s_old.astype(BF16)) * cross_ref[dr]
    kw = (kt.astype(F32) * wend_ref[dr]).astype(BF16)
    s_ref[dr] = decs_ref[...] * s_old + bd_ref[...] * _dot(kw, v)
    y_ref[...] = _group_norm(y, gm_ref[...], 1e-6).astype(y_ref.dtype)


def _fill_window_rows(ext_ref, shift_ref, prev_rows, rows, next_rows):
    r = rows.shape[0]
    ext_ref[0:HALO, :] = prev_rows
    ext_ref[HALO:HALO + r, :] = rows
    ext_ref[HALO + r:, :] = next_rows
    n = shift_ref.shape[1]
    for b in range(1, SUBLANES):
        shift_ref[b - 1] = ext_ref[b:b + n, :]


def _window_rows(ext_ref, shift_ref, o, n):
    a, b = divmod(o, SUBLANES)
    if b == 0:
        return ext_ref[o:o + n, :]
    return shift_ref[b - 1, SUBLANES * a:SUBLANES * a + n, :]


def _pool_mix(i, nc, seg_lens, halo_ok, pool_ref, pool_p, pool_n, wpool_ref, spool_ref, yc_ref, ext_ref,
              shift_ref):
    r, w = pool_ref.shape
    x = pool_ref[...].astype(F32)
    _fill_window_rows(ext_ref, shift_ref, pool_p[...].astype(F32) * halo_ok[0], x,
                      pool_n[...].astype(F32) * halo_ok[1])

    def sh(k):
        return _window_rows(ext_ref, shift_ref, HALO + k, r)

    w2 = sh(-1) + x
    w4 = w2 + sh(-2) + sh(1)
    w8 = w4 + sh(-4) + sh(-3) + sh(2) + sh(3)
    w16 = w8 + sh(-8) + sh(-7) + sh(-6) + sh(-5) + sh(4) + sh(5) + sh(6) + sh(7)
    grp = lax.broadcasted_iota(jnp.int32, (r, w), 1) >> 6
    wsum = jnp.where(grp == 0, w2, jnp.where(grp == 1, w4, jnp.where(grp == 2, w8, w16)))
    half = jnp.left_shift(1, grp)
    in_ctx = i < nc
    seg_len = jnp.where(in_ctx, seg_lens[0], seg_lens[1])
    pos = lax.broadcasted_iota(jnp.int32, (r, w), 0) + (i - jnp.where(in_ctx, 0, nc)) * r
    cnt = jnp.minimum(pos + half, seg_len) - jnp.maximum(pos - half, 0)
    dlt = wsum / cnt.astype(F32) - x
    yc_ref[...] = (_dot(dlt.astype(BF16), wpool_ref[...]) * spool_ref[...]).astype(yc_ref.dtype)


def _conv_mix(halo_ok, glu_ref, glu_p, glu_n, cw_ref, cb_ref, lng_ref, lnb_ref, wpw_ref, yd_ref, ext_ref,
              shift_ref):
    r = glu_ref.shape[0]
    w = ext_ref.shape[1]

    def glu(ref):
        v = ref[...].astype(F32)
        return v[:, :w] * jax.nn.sigmoid(v[:, w:])

    _fill_window_rows(ext_ref, shift_ref, glu(glu_p) * halo_ok[0], glu(glu_ref), glu(glu_n) * halo_ok[1])
    for r0 in range(0, r, CONV_ROWS):
        acc = jnp.zeros((CONV_ROWS, w), F32) + cb_ref[...]
        for k in range(CONV_K):
            acc = acc + _window_rows(ext_ref, shift_ref, HALO - CONV_K // 2 + k + r0,
                                     CONV_ROWS) * cw_ref[k:k + 1, :]
        mu = jnp.mean(acc, axis=-1, keepdims=True)
        d = acc - mu
        var = jnp.mean(d * d, axis=-1, keepdims=True)
        hn = d * lax.rsqrt(var + 1e-5) * lng_ref[...] + lnb_ref[...]
        yd_ref[r0:r0 + CONV_ROWS, :] = _dot(jax.nn.silu(hn).astype(BF16),
                                           wpw_ref[...]).astype(yd_ref.dtype)


def _bcd_kernel(nc, nt, seg_lens, qf_ref, ktf_ref, vf_ref, qb_ref, ktb_ref, vb_ref, dm_ref, wend_ref,
                cross_ref, decs_ref, bd_ref, gm_ref, glu_ref, glu_p, glu_n, pool_ref, pool_p, pool_n,
                wpool_ref, spool_ref, cw_ref, cb_ref, lng_ref, lnb_ref, wpw_ref,
                yf_ref, yb_ref, yc_ref, yd_ref, s_ref, ext_ref, shift_ref):
    i = pl.program_id(1)

    @pl.when(i == 0)
    def _init():
        s_ref[...] = jnp.zeros(s_ref.shape, F32)

    ret_tabs = (dm_ref, wend_ref, cross_ref, decs_ref, bd_ref, gm_ref, s_ref)
    halo_ok = (((i != 0) & (i != nc)).astype(F32), ((i != nc - 1) & (i != nt - 1)).astype(F32))
    _ret_direction(0, qf_ref, ktf_ref, vf_ref, yf_ref, *ret_tabs)
    _pool_mix(i, nc, seg_lens, halo_ok, pool_ref, pool_p, pool_n, wpool_ref, spool_ref, yc_ref, ext_ref,
              shift_ref)
    _ret_direction(1, qb_ref, ktb_ref, vb_ref, yb_ref, *ret_tabs)
    _conv_mix(halo_ok, glu_ref, glu_p, glu_n, cw_ref, cb_ref, lng_ref, lnb_ref, wpw_ref, yd_ref, ext_ref,
              shift_ref)


def _mixers_bcd(pn, pt, ret_tabs, cd_weights, nc, seg_lens):
    bsz, s, _ = pn.shape
    c = ROW_TILE
    nt = s // c
    w = MXU_TILE
    cb = lambda i: jnp.where(i < nc, nc - 1 - i, nt - 1 - (i - nc))
    hb = c // HALO
    prev = lambda i: jnp.maximum(i * hb - 1, 0)
    nxt = lambda i: jnp.minimum((i + 1) * hb, s // HALO - 1)
    const = lambda a: pl.BlockSpec(a.shape, lambda b, i: (0,) * a.ndim)
    tile = pl.BlockSpec((None, c, w), lambda b, i: (b, i, 0))
    sds = jax.ShapeDtypeStruct((bsz, s, w), BF16)
    return pl.pallas_call(
        functools.partial(_bcd_kernel, nc, nt, seg_lens),
        grid=(bsz, nt),
        in_specs=[pl.BlockSpec((None, c, w), lambda b, i: (b, i, 3)),
                  pl.BlockSpec((None, w, c), lambda b, i: (b, 1, i)),
                  pl.BlockSpec((None, c, w), lambda b, i: (b, i, 4)),
                  pl.BlockSpec((None, c, w), lambda b, i: (b, cb(i), 3)),
                  pl.BlockSpec((None, w, c), lambda b, i: (b, 1, cb(i))),
                  pl.BlockSpec((None, c, w), lambda b, i: (b, cb(i), 4))]
                 + [const(a) for a in ret_tabs]
                 + [pl.BlockSpec((None, c, 2 * w), lambda b, i: (b, i, 0)),
                    pl.BlockSpec((None, HALO, 2 * w), lambda b, i: (b, prev(i), 0)),
                    pl.BlockSpec((None, HALO, 2 * w), lambda b, i: (b, nxt(i), 0)),
                    pl.BlockSpec((None, c, w), lambda b, i: (b, i, 7)),
                    pl.BlockSpec((None, HALO, w), lambda b, i: (b, prev(i), 7)),
                    pl.BlockSpec((None, HALO, w), lambda b, i: (b, nxt(i), 7))]
                 + [const(a) for a in cd_weights],
        out_specs=[tile, pl.BlockSpec((None, c, w), lambda b, i: (b, cb(i), 0)), tile, tile],
        out_shape=[sds, sds, sds, sds],
        scratch_shapes=[pltpu.VMEM((2, w, w), F32),
                        pltpu.VMEM((c + 2 * HALO, w), F32),
                        pltpu.VMEM((SUBLANES - 1, c + 2 * HALO - SUBLANES, w), F32)],
        compiler_params=_cparams(("parallel", "arbitrary")),
        name="mixers_bcd",
    )(pn, pt, pn, pn, pt, pn, *ret_tabs, pn, pn, pn, pn, pn, pn, *cd_weights)


def _retention_tables(c):
    lg = jnp.asarray([math.log(1.0 - 2.0 ** (-5 - h)) for h in range(B_HEADS)], F32)
    j = jnp.arange(c, dtype=F32)
    dist = j[:, None] - j[None, :]
    df = jnp.where(dist >= 0, jnp.exp(lg[:, None, None] * jnp.maximum(dist, 0.0)), 0.0)
    dmask = jnp.stack([df, jnp.swapaxes(df, 1, 2)])
    row_head = (jnp.arange(MXU_TILE) % LANES) // 32
    col_head = jnp.arange(MXU_TILE) // 64
    wend_f = jnp.exp(lg[row_head][:, None] * (c - 1.0 - j)[None, :])
    wend_b = jnp.exp(lg[row_head][:, None] * j[None, :])
    cross_f = jnp.exp(lg[col_head][None, :] * (j + 1.0)[:, None])
    cross_b = jnp.exp(lg[col_head][None, :] * (c - j)[:, None])
    bdm = (row_head[:, None] == col_head[None, :]).astype(F32)
    decs = bdm * jnp.exp(lg * c)[col_head][None, :]
    gmat = ((col_head[:, None] == col_head[None, :]).astype(F32) / 64.0).astype(BF16)
    return (dmask, jnp.stack([wend_f, wend_b]), jnp.stack([cross_f, cross_b]), decs, bdm, gmat)


def _tail_kernel(h_ref, mod_ref, gpm_ref, gprf_ref, gpof_ref, ya_ref, yf_ref, yb_ref, gf_ref, gb_ref,
                 yc_ref, yd_ref, wo_ref, wg_ref, wu_ref, wd_ref, o_ref):
    rb = h_ref.shape[0] // TAIL_SUBBLOCKS
    rows = [pl.ds(j * rb, rb) for j in range(TAIL_SUBBLOCKS)]

    def mix_in(rs):
        yb = (jax.nn.silu(gf_ref[rs, :].astype(F32)) * yf_ref[rs, :].astype(F32)
              + jax.nn.silu(gb_ref[rs, :].astype(F32)) * yb_ref[rs, :].astype(F32))
        ycat = jnp.concatenate([ya_ref[rs, :], yb.astype(BF16), yc_ref[rs, :], yd_ref[rs, :]], axis=-1)
        return _dot(ycat, wo_ref[...])

    ys = [mix_in(rs) for rs in rows]
    xs, us = [], []
    for rs, y in zip(rows, ys):
        x = h_ref[rs, :] + mod_ref[2:3, :] * (_rms(y, 1e-6) * gpm_ref[...])
        u = _rms(x, 1e-6) * gprf_ref[...]
        xs.append(x)
        us.append((u * (1.0 + mod_ref[4:5, :]) + mod_ref[3:4, :]).astype(BF16))
    acts = [(jax.nn.silu(_dot(ub, wg_ref[...])) * _dot(ub, wu_ref[...])).astype(BF16) for ub in us]
    fs = [_dot(a, wd_ref[...]) for a in acts]
    for rs, x, f in zip(rows, xs, fs):
        o_ref[rs, :] = x + mod_ref[5:6, :] * (_rms(f, 1e-6) * gpof_ref[...])


def _tail(h, mod, g_post_mix, g_pre_ffn, g_post_ffn, ya, yf, yb, pn, yc, yd, w_out, wg, wu, wd, nc,
          latent_only):
    bsz, s, d = h.shape
    r = ROW_TILE
    nt = s // r
    out_rows = s - nc * r if latent_only else s
    out_tile = (lambda b, i: (b, jnp.maximum(i - nc, 0), 0)) if latent_only else (lambda b, i: (b, i, 0))
    w = MXU_TILE
    msel = lambda b, i: (jnp.where(i < nc, bsz, b), 0, 0)
    tile = lambda col: pl.BlockSpec((None, r, w), lambda b, i: (b, i, col))
    vec = pl.BlockSpec((1, d), lambda b, i: (0, 0))
    resident = lambda a: pl.BlockSpec(a.shape, lambda b, i: (0, 0), pipeline_mode=pl.Buffered(1))
    return pl.pallas_call(
        _tail_kernel,
        grid=(bsz, nt),
        in_specs=[pl.BlockSpec((None, r, d), lambda b, i: (b, i, 0)),
                  pl.BlockSpec((None, 6, d), msel),
                  vec, vec, vec,
                  tile(0), tile(0), tile(0), tile(5), tile(6), tile(0), tile(0),
                  resident(w_out), resident(wg), resident(wu), resident(wd)],
        out_specs=pl.BlockSpec((None, r, d), out_tile),
        out_shape=jax.ShapeDtypeStruct((bsz, out_rows, d), F32),
        compiler_params=_cparams(("parallel", "arbitrary")),
        name="out_ffn",
    )(h, mod, g_post_mix, g_pre_ffn, g_post_ffn, ya, yf, yb, pn, pn, yc, yd, w_out, wg, wu, wd)


def _rope_tables(t_len, ctx_len, dqk_a, dk_b):
    rows = t_len // GRID_W
    row = jnp.repeat(jnp.arange(rows, dtype=F32), GRID_W)
    col = jnp.tile(jnp.arange(GRID_W, dtype=F32), rows)

    def cs(d, reps):
        n_freq = d // 4
        inv = ROPE_BASE ** (-jnp.arange(n_freq, dtype=F32) / n_freq)
        ang = jnp.concatenate([row[:, None] * inv, col[:, None] * inv], axis=-1)
        cos = jnp.concatenate([jnp.ones((ctx_len, d // 2), F32), jnp.cos(ang)], axis=0)
        sin = jnp.concatenate([jnp.zeros((ctx_len, d // 2), F32), jnp.sin(ang)], axis=0)
        return jnp.tile(cos, (1, reps)), jnp.tile(sin, (1, reps))

    ca, sa = cs(dqk_a, LANES // (dqk_a // 2))
    cb, sb = cs(dk_b, LANES // (dk_b // 2))
    tab_n = jnp.concatenate([ca, sa, cb, sb], axis=1)
    return tab_n, tab_n.T


def _rotary_perm(n_groups, d):
    half, g, i = jnp.meshgrid(jnp.arange(2), jnp.arange(n_groups), jnp.arange(d // 2), indexing="ij")
    return (g * d + half * (d // 2) + i).reshape(-1)


def kernel(x, c, ctx, c_ctx, w_ada, b_ada, g_pre_mix, g_post_mix, g_pre_ffn, g_post_ffn, w_in, w_out, lam_q1, lam_k1, lam_q2, lam_k2, g_subln, w_pool, s_pool, conv_w, conv_b, conv_ln_g, conv_ln_b, w_conv_out, w_ffn_gate, w_ffn_up, w_ffn_down):
    bsz, t_len, d = x.shape
    ctx_len = ctx.shape[1]
    depth = w_in.shape[0]
    assert d == 8 * A_HEADS * 32 and t_len % ROW_TILE == 0 and ctx_len % ROW_TILE == 0
    assert t_len % GRID_W == 0 and bsz < 8
    qw = d // 4
    dqk_a = qw // (2 * A_HEADS)
    dk_b = qw // B_HEADS
    nc = ctx_len // ROW_TILE

    h = jnp.concatenate([ctx, x], axis=1)

    cond = jnp.zeros((8, d), F32).at[:bsz].set(c).at[bsz].set(c_ctx)
    mods = _modulation(cond, w_ada, b_ada).reshape(depth, 8, 6, d)

    pa = _rotary_perm(2 * A_HEADS, dqk_a)
    pb = _rotary_perm(B_HEADS, dk_b)
    col = lambda k: w_in[:, :, k * qw:(k + 1) * qw]
    k_a, v_a, k_b, v_b, q_a, q_b, g_f, g_b, pool = (col(k) for k in range(9))
    glu = w_in[:, :, 9 * qw:]
    w_nat = jnp.concatenate([glu, k_a[:, :, pa], q_b[:, :, pb], v_b, g_f, g_b, pool], axis=-1).astype(BF16)
    w_tr = jnp.swapaxes(jnp.concatenate([q_a[:, :, pa], k_b[:, :, pb], v_a], axis=-1), 1, 2).astype(BF16)
    w_out_b = w_out.astype(BF16)
    wg_b, wu_b, wd_b = w_ffn_gate.astype(BF16), w_ffn_up.astype(BF16), w_ffn_down.astype(BF16)
    w_pw_b = w_conv_out.astype(BF16)
    eye = jnp.eye(len(POOL_WINDOWS), dtype=F32)
    wpool_bd = jnp.einsum("lgce,gh->lgche", w_pool, eye).reshape(depth, qw, qw).astype(BF16)
    conv_w2 = jnp.pad(conv_w.reshape(depth, CONV_K, qw), ((0, 0), (0, 1), (0, 0)))
    lamv = jnp.stack([lam_q1, lam_k1, lam_q2, lam_k2], axis=1)
    g_col = g_subln[:, :, None]

    tab_n, tab_t = _rope_tables(t_len, ctx_len, dqk_a, dk_b)
    ret_tabs = _retention_tables(ROW_TILE)
    row = lambda a, l: a[l][None, :]

    for l in range(depth):
        lam_init = 0.8 - 0.6 * math.exp(-0.3 * l)
        mod = mods[l]
        pn, pt = _inproj(h, mod, row(g_pre_mix, l), w_nat[l], w_tr[l], tab_n, tab_t, nc,
                         dqk_a ** -0.5 * LOG2_E, dk_b ** -0.5)
        ya = _attention(pn, pt, ctx_len, lamv[l], g_col[l], nc, lam_init)
        cd_weights = (wpool_bd[l], row(s_pool, l), conv_w2[l], row(conv_b, l), row(conv_ln_g, l),
                      row(conv_ln_b, l), w_pw_b[l])
        yf, yb, yc, yd = _mixers_bcd(pn, pt, ret_tabs, cd_weights, nc, (ctx_len, t_len))
        h = _tail(h, mod, row(g_post_mix, l), row(g_pre_ffn, l), row(g_post_ffn, l), ya, yf, yb, pn,
                  yc, yd, w_out_b[l], wg_b[l], wu_b[l], wd_b[l], nc, latent_only=l == depth - 1)
    return h
```

```python
import functools
import math

import jax
import jax.numpy as jnp
from jax import lax
from jax.experimental import pallas as pl
from jax.experimental.pallas import tpu as pltpu

F32 = jnp.float32
BF16 = jnp.bfloat16

GRID_W = 64
ROPE_BASE = 10000.0
A_HEADS = 4
B_HEADS = 4
POOL_WINDOWS = (2, 4, 8, 16)
CONV_K = 31
LOG2_E = math.log2(math.e)

LANES = 128
SUBLANES = 8
MXU_TILE = 256
ROW_TILE = 256
HALO = 16
VMEM_LIMIT = 56 * 1024 * 1024
TAIL_SUBBLOCKS = 2
ATTN_KEY_TILE = 4096
ATTN_SKEW = 3
ATTN_HEADROOM = 8.0
CONV_ROWS = 64


def _cparams(sem):
    return pltpu.CompilerParams(dimension_semantics=sem, vmem_limit_bytes=VMEM_LIMIT)


def _rms(x, eps):
    return x * lax.rsqrt(jnp.mean(x * x, axis=-1, keepdims=True) + eps)


def _dot(a, b):
    return jnp.dot(a, b, preferred_element_type=F32)


def _mod_kernel(c_ref, w_ref, b_ref, o_ref):
    a = jax.nn.silu(c_ref[...])
    o_ref[...] = jnp.dot(a, w_ref[...], preferred_element_type=F32,
                         precision=lax.Precision.HIGHEST) + b_ref[...]


def _modulation(cond, w_ada, b_ada):
    n_layers, d, n6 = w_ada.shape
    tn = 2048
    return pl.pallas_call(
        _mod_kernel,
        grid=(n_layers, n6 // tn),
        in_specs=[pl.BlockSpec((8, d), lambda l, j: (0, 0)),
                  pl.BlockSpec((None, d, tn), lambda l, j: (l, 0, j)),
                  pl.BlockSpec((None, 1, tn), lambda l, j: (l, 0, j))],
        out_specs=pl.BlockSpec((None, 8, tn), lambda l, j: (l, 0, j)),
        out_shape=jax.ShapeDtypeStruct((n_layers, 8, n6), F32),
        compiler_params=_cparams(("parallel", "parallel")),
        name="adaln_mod",
    )(cond, w_ada, b_ada.reshape(n_layers, 1, n6))


def _inproj_kernel(a_scale, b_scale, h_ref, mod_ref, g_ref, wn_ref, wt_ref, tn_ref, tt_ref,
                   pn_ref, pt_ref):
    mod = mod_ref[...]
    u = _rms(h_ref[...], 1e-6) * g_ref[...]
    ub = (u * (1.0 + mod[1:2]) + mod[0:1]).astype(BF16)

    def nat(c0, c1):
        return _dot(ub, wn_ref[:, c0:c1])

    def tr(r0, r1):
        return lax.dot_general(wt_ref[r0:r1, :], ub, (((1,), (1,)), ((), ())),
                               preferred_element_type=F32)

    pn_ref[:, 0:512] = nat(0, 512).astype(BF16)
    ka = nat(512, 768)
    x1, x2 = ka[:, :LANES], ka[:, LANES:]
    cs, sn = tn_ref[:, 0:128], tn_ref[:, 128:256]
    pn_ref[:, 512:640] = (x1 * cs - x2 * sn).astype(BF16)
    pn_ref[:, 640:768] = (x1 * sn + x2 * cs).astype(BF16)
    qb = nat(768, 1024)
    x1, x2 = qb[:, :LANES], qb[:, LANES:]
    cs, sn = tn_ref[:, 256:384], tn_ref[:, 384:512]
    pn_ref[:, 768:896] = (x1 * cs - x2 * sn).astype(BF16)
    pn_ref[:, 896:1024] = (x1 * sn + x2 * cs).astype(BF16)
    pn_ref[:, 1024:2048] = nat(1024, 2048).astype(BF16)

    qa = tr(0, 256) * a_scale
    x1, x2 = qa[:LANES], qa[LANES:]
    cs, sn = tt_ref[0:128, :], tt_ref[128:256, :]
    pt_ref[0:128, :] = (x1 * cs - x2 * sn).astype(BF16)
    pt_ref[128:256, :] = (x1 * sn + x2 * cs).astype(BF16)
    kb = tr(256, 512) * b_scale
    x1, x2 = kb[:LANES], kb[LANES:]
    cs, sn = tt_ref[256:384, :], tt_ref[384:512, :]
    pt_ref[256:384, :] = (x1 * cs - x2 * sn).astype(BF16)
    pt_ref[384:512, :] = (x1 * sn + x2 * cs).astype(BF16)
    pt_ref[512:768, :] = tr(512, 768).astype(BF16)


def _inproj(h, mod, g_pre, w_nat, w_tr, tab_n, tab_t, nc, a_scale, b_scale):
    bsz, s, d = h.shape
    nt = s // ROW_TILE
    nn, ntr = w_nat.shape[1], w_tr.shape[0]
    msel = lambda b, i: (jnp.where(i < nc, bsz, b), 0, 0)
    return pl.pallas_call(
        functools.partial(_inproj_kernel, a_scale, b_scale),
        grid=(bsz, nt),
        in_specs=[pl.BlockSpec((None, ROW_TILE, d), lambda b, i: (b, i, 0)),
                  pl.BlockSpec((None, 6, d), msel),
                  pl.BlockSpec((1, d), lambda b, i: (0, 0)),
                  pl.BlockSpec((d, nn), lambda b, i: (0, 0)),
                  pl.BlockSpec((ntr, d), lambda b, i: (0, 0)),
                  pl.BlockSpec((ROW_TILE, 4 * LANES), lambda b, i: (i, 0)),
                  pl.BlockSpec((4 * LANES, ROW_TILE), lambda b, i: (0, i))],
        out_specs=[pl.BlockSpec((None, ROW_TILE, nn), lambda b, i: (b, i, 0)),
                   pl.BlockSpec((None, ntr, ROW_TILE), lambda b, i: (b, 0, i))],
        out_shape=[jax.ShapeDtypeStruct((bsz, s, nn), BF16),
                   jax.ShapeDtypeStruct((bsz, ntr, s), BF16)],
        compiler_params=_cparams(("parallel", "parallel")),
        name="in_proj",
    )(h, mod, g_pre, w_nat, w_tr, tab_n, tab_t)


def _attn_kernel(nc, n_ksteps, lam_init, qt_ref, kc_ref, vtc_ref, kl_ref, vtl_ref, lamv_ref, g_ref,
                 o_ref, qz_ref, m_ref, acc_ref, tmax_ref, pv_ref, l_ref, lp_ref):
    qi = pl.program_id(1)
    ki = pl.program_id(2)
    n_groups = 2 * A_HEADS
    dv = vtc_ref.shape[0] // A_HEADS

    def attend(k_ref, vt_ref):
        k = k_ref[...]
        s_q = [_dot(k, qz_ref[g]) for g in range(ATTN_SKEW)]
        for g in range(n_groups):
            hd = g // 2
            s = s_q.pop(0)
            if g + ATTN_SKEW < n_groups:
                s_q.append(_dot(k, qz_ref[g + ATTN_SKEW]))
            m_old = m_ref[g]
            m_new = jnp.maximum(m_old, jnp.max(s, axis=0, keepdims=True))
            p = jnp.exp2(s - m_new)
            alpha = jnp.exp2(m_old - m_new)
            l_ref[g] = alpha * l_ref[g] + jnp.sum(p, axis=0, keepdims=True)
            acc_ref[g] = alpha * acc_ref[g] + _dot(vt_ref[hd * dv:(hd + 1) * dv, :], p.astype(BF16))
            m_ref[g] = m_new

    def attend_lagged(k_ref, vt_ref):
        k = k_ref[...]
        s_q = [_dot(k, qz_ref[g]) for g in range(ATTN_SKEW)]
        excess = None
        for g in range(n_groups):
            hd = g // 2
            s = s_q.pop(0)
            if g + ATTN_SKEW < n_groups:
                s_q.append(_dot(k, qz_ref[g + ATTN_SKEW]))
            m_fix = m_ref[g]
            p = jnp.exp2(s - m_fix)
            t_max = jnp.max(s, axis=0, keepdims=True)
            tmax_ref[g] = t_max
            lp_ref[g] = jnp.sum(p, axis=0, keepdims=True)
            excess = t_max - m_fix if excess is None else jnp.maximum(excess, t_max - m_fix)
            pv_ref[g] = _dot(vt_ref[hd * dv:(hd + 1) * dv, :], p.astype(BF16))
        return jnp.max(excess)

    def commit_lagged():
        for g in range(n_groups):
            m_old = m_ref[g]
            m_new = jnp.maximum(m_old, tmax_ref[g])
            alpha = jnp.exp2(m_old - m_new)
            acc_ref[g] = (acc_ref[g] + pv_ref[g]) * alpha
            l_ref[g] = (l_ref[g] + lp_ref[g]) * alpha
            m_ref[g] = m_new

    @pl.when(ki == 0)
    def _first():
        qt = qt_ref[...]
        rg = (lax.broadcasted_iota(jnp.int32, qt.shape, 0) & (LANES - 1)) >> 4
        for g in range(n_groups):
            qz_ref[g] = jnp.where(rg == g, qt, jnp.zeros_like(qt))
        m_ref[...] = jnp.full(m_ref.shape, -1e30, F32)
        acc_ref[...] = jnp.zeros(acc_ref.shape, F32)
        l_ref[...] = jnp.zeros(l_ref.shape, F32)
        attend(kc_ref, vtc_ref)

    @pl.when((ki > 0) & (qi >= nc))
    def _latent():
        within = attend_lagged(kl_ref, vtl_ref) <= ATTN_HEADROOM

        @pl.when(within)
        def _commit():
            commit_lagged()

        @pl.when(jnp.logical_not(within))
        def _redo():
            attend(kl_ref, vtl_ref)

    @pl.when(ki == n_ksteps - 1)
    def _fin():
        lv = lamv_ref[...]
        a1 = jnp.sum(lv[0:1] * lv[1:2], axis=-1, keepdims=True)
        a2 = jnp.sum(lv[2:3] * lv[3:4], axis=-1, keepdims=True)
        lam = jnp.exp(a1) - jnp.exp(a2) + lam_init
        outs = []
        for hd in range(A_HEADS):
            o = acc_ref[2 * hd] / l_ref[2 * hd] - lam * (acc_ref[2 * hd + 1] / l_ref[2 * hd + 1])
            ms = jnp.mean(o * o, axis=0, keepdims=True)
            outs.append(o * lax.rsqrt(ms + 1e-5) * g_ref[...] * (1.0 - lam_init))
        o_ref[...] = jnp.concatenate(outs, axis=0).T.astype(o_ref.dtype)


def _attention(pn, pt, ctx_len, lamv, g_col, nc, lam_init):
    bsz, s, _ = pn.shape
    t_len = s - ctx_len
    nq = s // ROW_TILE
    tk = min(ATTN_KEY_TILE, t_len)
    assert t_len % tk == 0
    n_ksteps = 1 + t_len // tk
    w = MXU_TILE
    dv_ext = w // A_HEADS
    kl = lambda qi, ki: pl.multiple_of(ctx_len + jnp.where(qi < nc, 0, jnp.maximum(ki - 1, 0)) * tk, ROW_TILE)
    return pl.pallas_call(
        functools.partial(_attn_kernel, nc, n_ksteps, lam_init),
        grid=(bsz, nq, n_ksteps),
        in_specs=[pl.BlockSpec((None, w, ROW_TILE), lambda b, qi, ki: (b, 0, qi)),
                  pl.BlockSpec((None, ctx_len, w), lambda b, qi, ki: (b, 0, 2)),
                  pl.BlockSpec((None, w, ctx_len), lambda b, qi, ki: (b, 2, 0)),
                  pl.BlockSpec((None, pl.Element(tk), pl.Element(w)), lambda b, qi, ki: (b, kl(qi, ki), 2 * w)),
                  pl.BlockSpec((None, pl.Element(w), pl.Element(tk)), lambda b, qi, ki: (b, 2 * w, kl(qi, ki))),
                  pl.BlockSpec(lamv.shape, lambda b, qi, ki: (0, 0)),
                  pl.BlockSpec(g_col.shape, lambda b, qi, ki: (0, 0))],
        out_specs=pl.BlockSpec((None, ROW_TILE, w), lambda b, qi, ki: (b, qi, 0)),
        out_shape=jax.ShapeDtypeStruct((bsz, s, w), BF16),
        scratch_shapes=[pltpu.VMEM((2 * A_HEADS, w, ROW_TILE), BF16),
                        pltpu.VMEM((2 * A_HEADS, 1, ROW_TILE), F32),
                        pltpu.VMEM((2 * A_HEADS, dv_ext, ROW_TILE), F32),
                        pltpu.VMEM((2 * A_HEADS, 1, ROW_TILE), F32),
                        pltpu.VMEM((2 * A_HEADS, dv_ext, ROW_TILE), F32),
                        pltpu.VMEM((2 * A_HEADS, 1, ROW_TILE), F32),
                        pltpu.VMEM((2 * A_HEADS, 1, ROW_TILE), F32)],
        compiler_params=_cparams(("parallel", "parallel", "arbitrary")),
        name="diff_attn",
    )(pt, pn, pt, pn, pt, lamv, g_col)


def _group_norm(y, gmat, eps):
    yh = y.astype(BF16)
    yl = (y - yh.astype(F32)).astype(BF16)
    d = y - (_dot(yh, gmat) + _dot(yl, gmat))
    return d * lax.rsqrt(_dot((d * d).astype(BF16), gmat) + eps)


def _ret_direction(dr, q_ref, kt_ref, v_ref, y_ref, dm_ref, wend_ref, cross_ref, decs_ref, bd_ref,
                   gm_ref, s_ref):
    q = q_ref[...]
    kt = kt_ref[...]
    v = v_ref[...]
    rh = (lax.broadcasted_iota(jnp.int32, kt.shape, 0) & (LANES - 1)) >> 5
    ch = lax.broadcasted_iota(jnp.int32, (q.shape[0], v.shape[1]), 1) >> 6
    y = jnp.zeros((q.shape[0], v.shape[1]), F32)
    for hd in range(B_HEADS):
        ktz = jnp.where(rh == hd, kt, jnp.zeros_like(kt))
        p = (_dot(q, ktz) * dm_ref[dr, hd]).astype(BF16)
        y = jnp.where(ch == hd, _dot(p, v), y)
    s_old = s_ref[dr]
    y = y + _dot(q, s_old.astype(BF16)) * cross_ref[dr]
    kw = (kt.astype(F32) * wend_ref[dr]).astype(BF16)
    s_ref[dr] = decs_ref[...] * s_old + bd_ref[...] * _dot(kw, v)
    y_ref[...] = _group_norm(y, gm_ref[...], 1e-6).astype(y_ref.dtype)


def _fill_window_rows(ext_ref, shift_ref, prev_rows, rows, next_rows):
    r = rows.shape[0]
    ext_ref[0:HALO, :] = prev_rows
    ext_ref[HALO:HALO + r, :] = rows
    ext_ref[HALO + r:, :] = next_rows
    n = shift_ref.shape[1]
    for b in range(1, SUBLANES):
        shift_ref[b - 1] = ext_ref[b:b + n, :]


def _window_rows(ext_ref, shift_ref, o, n):
    a, b = divmod(o, SUBLANES)
    if b == 0:
        return ext_ref[o:o + n, :]
    return shift_ref[b - 1, SUBLANES * a:SUBLANES * a + n, :]


def _pool_mix(i, nc, seg_lens, halo_ok, pool_ref, pool_p, pool_n, wpool_ref, spool_ref, yc_ref, ext_ref,
              shift_ref):
    r, w = pool_ref.shape
    x = pool_ref[...].astype(F32)
    _fill_window_rows(ext_ref, shift_ref, pool_p[...].astype(F32) * halo_ok[0], x,
                      pool_n[...].astype(F32) * halo_ok[1])

    def sh(k):
        return _window_rows(ext_ref, shift_ref, HALO + k, r)

    w2 = sh(-1) + x
    w4 = w2 + sh(-2) + sh(1)
    w8 = w4 + sh(-4) + sh(-3) + sh(2) + sh(3)
    w16 = w8 + sh(-8) + sh(-7) + sh(-6) + sh(-5) + sh(4) + sh(5) + sh(6) + sh(7)
    grp = lax.broadcasted_iota(jnp.int32, (r, w), 1) >> 6
    wsum = jnp.where(grp == 0, w2, jnp.where(grp == 1, w4, jnp.where(grp == 2, w8, w16)))
    half = jnp.left_shift(1, grp)
    in_ctx = i < nc
    seg_len = jnp.where(in_ctx, seg_lens[0], seg_lens[1])
    pos = lax.broadcasted_iota(jnp.int32, (r, w), 0) + (i - jnp.where(in_ctx, 0, nc)) * r
    cnt = jnp.minimum(pos + half, seg_len) - jnp.maximum(pos - half, 0)
    dlt = wsum / cnt.astype(F32) - x
    yc_ref[...] = (_dot(dlt.astype(BF16), wpool_ref[...]) * spool_ref[...]).astype(yc_ref.dtype)


def _conv_mix(halo_ok, glu_ref, glu_p, glu_n, cw_ref, cb_ref, lng_ref, lnb_ref, wpw_ref, yd_ref, ext_ref,
              shift_ref):
    r = glu_ref.shape[0]
    w = ext_ref.shape[1]

    def glu(ref):
        v = ref[...].astype(F32)
        return v[:, :w] * jax.nn.sigmoid(v[:, w:])

    _fill_window_rows(ext_ref, shift_ref, glu(glu_p) * halo_ok[0], glu(glu_ref), glu(glu_n) * halo_ok[1])
    for r0 in range(0, r, CONV_ROWS):
        acc = jnp.zeros((CONV_ROWS, w), F32) + cb_ref[...]
        for k in range(CONV_K):
            acc = acc + _window_rows(ext_ref, shift_ref, HALO - CONV_K // 2 + k + r0,
                                     CONV_ROWS) * cw_ref[k:k + 1, :]
        mu = jnp.mean(acc, axis=-1, keepdims=True)
        d = acc - mu
        var = jnp.mean(d * d, axis=-1, keepdims=True)
        hn = d * lax.rsqrt(var + 1e-5) * lng_ref[...] + lnb_ref[...]
        yd_ref[r0:r0 + CONV_ROWS, :] = _dot(jax.nn.silu(hn).astype(BF16),
                                           wpw_ref[...]).astype(yd_ref.dtype)


def _bcd_kernel(nc, nt, seg_lens, qf_ref, ktf_ref, vf_ref, qb_ref, ktb_ref, vb_ref, dm_ref, wend_ref,
                cross_ref, decs_ref, bd_ref, gm_ref, glu_ref, glu_p, glu_n, pool_ref, pool_p, pool_n,
                wpool_ref, spool_ref, cw_ref, cb_ref, lng_ref, lnb_ref, wpw_ref,
                yf_ref, yb_ref, yc_ref, yd_ref, s_ref, ext_ref, shift_ref):
    i = pl.program_id(1)

    @pl.when(i == 0)
    def _init():
        s_ref[...] = jnp.zeros(s_ref.shape, F32)

    ret_tabs = (dm_ref, wend_ref, cross_ref, decs_ref, bd_ref, gm_ref, s_ref)
    halo_ok = (((i != 0) & (i != nc)).astype(F32), ((i != nc - 1) & (i != nt - 1)).astype(F32))
    _ret_direction(0, qf_ref, ktf_ref, vf_ref, yf_ref, *ret_tabs)
    _pool_mix(i, nc, seg_lens, halo_ok, pool_ref, pool_p, pool_n, wpool_ref, spool_ref, yc_ref, ext_ref,
              shift_ref)
    _ret_direction(1, qb_ref, ktb_ref, vb_ref, yb_ref, *ret_tabs)
    _conv_mix(halo_ok, glu_ref, glu_p, glu_n, cw_ref, cb_ref, lng_ref, lnb_ref, wpw_ref, yd_ref, ext_ref,
              shift_ref)


def _mixers_bcd(pn, pt, ret_tabs, cd_weights, nc, seg_lens):
    bsz, s, _ = pn.shape
    c = ROW_TILE
    nt = s // c
    w = MXU_TILE
    cb = lambda i: jnp.where(i < nc, nc - 1 - i, nt - 1 - (i - nc))
    hb = c // HALO
    prev = lambda i: jnp.maximum(i * hb - 1, 0)
    nxt = lambda i: jnp.minimum((i + 1) * hb, s // HALO - 1)
    const = lambda a: pl.BlockSpec(a.shape, lambda b, i: (0,) * a.ndim)
    tile = pl.BlockSpec((None, c, w), lambda b, i: (b, i, 0))
    sds = jax.ShapeDtypeStruct((bsz, s, w), BF16)
    return pl.pallas_call(
        functools.partial(_bcd_kernel, nc, nt, seg_lens),
        grid=(bsz, nt),
        in_specs=[pl.BlockSpec((None, c, w), lambda b, i: (b, i, 3)),
                  pl.BlockSpec((None, w, c), lambda b, i: (b, 1, i)),
                  pl.BlockSpec((None, c, w), lambda b, i: (b, i, 4)),
                  pl.BlockSpec((None, c, w), lambda b, i: (b, cb(i), 3)),
                  pl.BlockSpec((None, w, c), lambda b, i: (b, 1, cb(i))),
                  pl.BlockSpec((None, c, w), lambda b, i: (b, cb(i), 4))]
                 + [const(a) for a in ret_tabs]
                 + [pl.BlockSpec((None, c, 2 * w), lambda b, i: (b, i, 0)),
                    pl.BlockSpec((None, HALO, 2 * w), lambda b, i: (b, prev(i), 0)),
                    pl.BlockSpec((None, HALO, 2 * w), lambda b, i: (b, nxt(i), 0)),
                    pl.BlockSpec((None, c, w), lambda b, i: (b, i, 7)),
                    pl.BlockSpec((None, HALO, w), lambda b, i: (b, prev(i), 7)),
                    pl.BlockSpec((None, HALO, w), lambda b, i: (b, nxt(i), 7))]
                 + [const(a) for a in cd_weights],
        out_specs=[tile, pl.BlockSpec((None, c, w), lambda b, i: (b, cb(i), 0)), tile, tile],
        out_shape=[sds, sds, sds, sds],
        scratch_shapes=[pltpu.VMEM((2, w, w), F32),
                        pltpu.VMEM((c + 2 * HALO, w), F32),
                        pltpu.VMEM((SUBLANES - 1, c + 2 * HALO - SUBLANES, w), F32)],
        compiler_params=_cparams(("parallel", "arbitrary")),
        name="mixers_bcd",
    )(pn, pt, pn, pn, pt, pn, *ret_tabs, pn, pn, pn, pn, pn, pn, *cd_weights)


def _retention_tables(c):
    lg = jnp.asarray([math.log(1.0 - 2.0 ** (-5 - h)) for h in range(B_HEADS)], F32)
    j = jnp.arange(c, dtype=F32)
    dist = j[:, None] - j[None, :]
    df = jnp.where(dist >= 0, jnp.exp(lg[:, None, None] * jnp.maximum(dist, 0.0)), 0.0)
    dmask = jnp.stack([df, jnp.swapaxes(df, 1, 2)])
    row_head = (jnp.arange(MXU_TILE) % LANES) // 32
    col_head = jnp.arange(MXU_TILE) // 64
    wend_f = jnp.exp(lg[row_head][:, None] * (c - 1.0 - j)[None, :])
    wend_b = jnp.exp(lg[row_head][:, None] * j[None, :])
    cross_f = jnp.exp(lg[col_head][None, :] * (j + 1.0)[:, None])
    cross_b = jnp.exp(lg[col_head][None, :] * (c - j)[:, None])
    bdm = (row_head[:, None] == col_head[None, :]).astype(F32)
    decs = bdm * jnp.exp(lg * c)[col_head][None, :]
    gmat = ((col_head[:, None] == col_head[None, :]).astype(F32) / 64.0).astype(BF16)
    return (dmask, jnp.stack([wend_f, wend_b]), jnp.stack([cross_f, cross_b]), decs, bdm, gmat)


def _tail_kernel(h_ref, mod_ref, gpm_ref, gprf_ref, gpof_ref, ya_ref, yf_ref, yb_ref, gf_ref, gb_ref,
                 yc_ref, yd_ref, wo_ref, wg_ref, wu_ref, wd_ref, o_ref):
    rb = h_ref.shape[0] // TAIL_SUBBLOCKS
    rows = [pl.ds(j * rb, rb) for j in range(TAIL_SUBBLOCKS)]

    def mix_in(rs):
        yb = (jax.nn.silu(gf_ref[rs, :].astype(F32)) * yf_ref[rs, :].astype(F32)
              + jax.nn.silu(gb_ref[rs, :].astype(F32)) * yb_ref[rs, :].astype(F32))
        ycat = jnp.concatenate([ya_ref[rs, :], yb.astype(BF16), yc_ref[rs, :], yd_ref[rs, :]], axis=-1)
        return _dot(ycat, wo_ref[...])

    ys = [mix_in(rs) for rs in rows]
    xs, us = [], []
    for rs, y in zip(rows, ys):
        x = h_ref[rs, :] + mod_ref[2:3, :] * (_rms(y, 1e-6) * gpm_ref[...])
        u = _rms(x, 1e-6) * gprf_ref[...]
        xs.append(x)
        us.append((u * (1.0 + mod_ref[4:5, :]) + mod_ref[3:4, :]).astype(BF16))
    acts = [(jax.nn.silu(_dot(ub, wg_ref[...])) * _dot(ub, wu_ref[...])).astype(BF16) for ub in us]
    fs = [_dot(a, wd_ref[...]) for a in acts]
    for rs, x, f in zip(rows, xs, fs):
        o_ref[rs, :] = x + mod_ref[5:6, :] * (_rms(f, 1e-6) * gpof_ref[...])


def _tail(h, mod, g_post_mix, g_pre_ffn, g_post_ffn, ya, yf, yb, pn, yc, yd, w_out, wg, wu, wd, nc,
          latent_only):
    bsz, s, d = h.shape
    r = ROW_TILE
    nt = s // r
    out_rows = s - nc * r if latent_only else s
    out_tile = (lambda b, i: (b, jnp.maximum(i - nc, 0), 0)) if latent_only else (lambda b, i: (b, i, 0))
    w = MXU_TILE
    msel = lambda b, i: (jnp.where(i < nc, bsz, b), 0, 0)
    tile = lambda col: pl.BlockSpec((None, r, w), lambda b, i: (b, i, col))
    vec = pl.BlockSpec((1, d), lambda b, i: (0, 0))
    resident = lambda a: pl.BlockSpec(a.shape, lambda b, i: (0, 0), pipeline_mode=pl.Buffered(1))
    return pl.pallas_call(
        _tail_kernel,
        grid=(bsz, nt),
        in_specs=[pl.BlockSpec((None, r, d), lambda b, i: (b, i, 0)),
                  pl.BlockSpec((None, 6, d), msel),
                  vec, vec, vec,
                  tile(0), tile(0), tile(0), tile(5), tile(6), tile(0), tile(0),
                  resident(w_out), resident(wg), resident(wu), resident(wd)],
        out_specs=pl.BlockSpec((None, r, d), out_tile),
        out_shape=jax.ShapeDtypeStruct((bsz, out_rows, d), F32),
        compiler_params=_cparams(("parallel", "arbitrary")),
        name="out_ffn",
    )(h, mod, g_post_mix, g_pre_ffn, g_post_ffn, ya, yf, yb, pn, pn, yc, yd, w_out, wg, wu, wd)


def _rope_tables(t_len, ctx_len, dqk_a, dk_b):
    rows = t_len // GRID_W
    row = jnp.repeat(jnp.arange(rows, dtype=F32), GRID_W)
    col = jnp.tile(jnp.arange(GRID_W, dtype=F32), rows)

    def cs(d, reps):
        n_freq = d // 4
        inv = ROPE_BASE ** (-jnp.arange(n_freq, dtype=F32) / n_freq)
        ang = jnp.concatenate([row[:, None] * inv, col[:, None] * inv], axis=-1)
        cos = jnp.concatenate([jnp.ones((ctx_len, d // 2), F32), jnp.cos(ang)], axis=0)
        sin = jnp.concatenate([jnp.zeros((ctx_len, d // 2), F32), jnp.sin(ang)], axis=0)
        return jnp.tile(cos, (1, reps)), jnp.tile(sin, (1, reps))

    ca, sa = cs(dqk_a, LANES // (dqk_a // 2))
    cb, sb = cs(dk_b, LANES // (dk_b // 2))
    tab_n = jnp.concatenate([ca, sa, cb, sb], axis=1)
    return tab_n, tab_n.T


def _rotary_perm(n_groups, d):
    half, g, i = jnp.meshgrid(jnp.arange(2), jnp.arange(n_groups), jnp.arange(d // 2), indexing="ij")
    return (g * d + half * (d // 2) + i).reshape(-1)


def kernel(x, c, ctx, c_ctx, w_ada, b_ada, g_pre_mix, g_post_mix, g_pre_ffn, g_post_ffn, w_in, w_out, lam_q1, lam_k1, lam_q2, lam_k2, g_subln, w_pool, s_pool, conv_w, conv_b, conv_ln_g, conv_ln_b, w_conv_out, w_ffn_gate, w_ffn_up, w_ffn_down):
    bsz, t_len, d = x.shape
    ctx_len = ctx.shape[1]
    depth = w_in.shape[0]
    assert d == 8 * A_HEADS * 32 and t_len % ROW_TILE == 0 and ctx_len % ROW_TILE == 0
    assert t_len % GRID_W == 0 and bsz < 8
    qw = d // 4
    dqk_a = qw // (2 * A_HEADS)
    dk_b = qw // B_HEADS
    nc = ctx_len // ROW_TILE

    h = jnp.concatenate([ctx, x], axis=1)

    cond = jnp.zeros((8, d), F32).at[:bsz].set(c).at[bsz].set(c_ctx)
    mods = _modulation(cond, w_ada, b_ada).reshape(depth, 8, 6, d)

    pa = _rotary_perm(2 * A_HEADS, dqk_a)
    pb = _rotary_perm(B_HEADS, dk_b)
    col = lambda k: w_in[:, :, k * qw:(k + 1) * qw]
    k_a, v_a, k_b, v_b, q_a, q_b, g_f, g_b, pool = (col(k) for k in range(9))
    glu = w_in[:, :, 9 * qw:]
    w_nat = jnp.concatenate([glu, k_a[:, :, pa], q_b[:, :, pb], v_b, g_f, g_b, pool], axis=-1).astype(BF16)
    w_tr = jnp.swapaxes(jnp.concatenate([q_a[:, :, pa], k_b[:, :, pb], v_a], axis=-1), 1, 2).astype(BF16)
    w_out_b = w_out.astype(BF16)
    wg_b, wu_b, wd_b = w_ffn_gate.astype(BF16), w_ffn_up.astype(BF16), w_ffn_down.astype(BF16)
    w_pw_b = w_conv_out.astype(BF16)
    eye = jnp.eye(len(POOL_WINDOWS), dtype=F32)
    wpool_bd = jnp.einsum("lgce,gh->lgche", w_pool, eye).reshape(depth, qw, qw).astype(BF16)
    conv_w2 = jnp.pad(conv_w.reshape(depth, CONV_K, qw), ((0, 0), (0, 1), (0, 0)))
    lamv = jnp.stack([lam_q1, lam_k1, lam_q2, lam_k2], axis=1)
    g_col = g_subln[:, :, None]

    tab_n, tab_t = _rope_tables(t_len, ctx_len, dqk_a, dk_b)
    ret_tabs = _retention_tables(ROW_TILE)
    row = lambda a, l: a[l][None, :]

    for l in range(depth):
        lam_init = 0.8 - 0.6 * math.exp(-0.3 * l)
        mod = mods[l]
        pn, pt = _inproj(h, mod, row(g_pre_mix, l), w_nat[l], w_tr[l], tab_n, tab_t, nc,
                         dqk_a ** -0.5 * LOG2_E, dk_b ** -0.5)
        ya = _attention(pn, pt, ctx_len, lamv[l], g_col[l], nc, lam_init)
        cd_weights = (wpool_bd[l], row(s_pool, l), conv_w2[l], row(conv_b, l), row(conv_ln_g, l),
                      row(conv_ln_b, l), w_pw_b[l])
        yf, yb, yc, yd = _mixers_bcd(pn, pt, ret_tabs, cd_weights, nc, (ctx_len, t_len))
        h = _tail(h, mod, row(g_post_mix, l), row(g_pre_ffn, l), row(g_post_ffn, l), ya, yf, yb, pn,
                  yc, yd, w_out_b[l], wg_b[l], wu_b[l], wd_b[l], nc, latent_only=l == depth - 1)
    return h
```

```python
import functools
import math

import jax
import jax.numpy as jnp
from jax import lax
from jax.experimental import pallas as pl
from jax.experimental.pallas import tpu as pltpu

F32 = jnp.float32
BF16 = jnp.bfloat16

GRID_W = 64
ROPE_BASE = 10000.0
A_HEADS = 4
B_HEADS = 4
POOL_WINDOWS = (2, 4, 8, 16)
CONV_K = 31
LOG2_E = math.log2(math.e)

LANES = 128
SUBLANES = 8
MXU_TILE = 256
ROW_TILE = 256
HALO = 16
VMEM_LIMIT = 56 * 1024 * 1024
TAIL_SUBBLOCKS = 2
ATTN_KEY_TILE = 4096
ATTN_SKEW = 3
ATTN_HEADROOM = 8.0
CONV_ROWS = 64


def _cparams(sem):
    return pltpu.CompilerParams(dimension_semantics=sem, vmem_limit_bytes=VMEM_LIMIT)


def _rms(x, eps):
    return x * lax.rsqrt(jnp.mean(x * x, axis=-1, keepdims=True) + eps)


def _dot(a, b):
    return jnp.dot(a, b, preferred_element_type=F32)


def _mod_kernel(c_ref, w_ref, b_ref, o_ref):
    a = jax.nn.silu(c_ref[...])
    o_ref[...] = jnp.dot(a, w_ref[...], preferred_element_type=F32,
                         precision=lax.Precision.HIGHEST) + b_ref[...]


def _modulation(cond, w_ada, b_ada):
    n_layers, d, n6 = w_ada.shape
    tn = 2048
    return pl.pallas_call(
        _mod_kernel,
        grid=(n_layers, n6 // tn),
        in_specs=[pl.BlockSpec((8, d), lambda l, j: (0, 0)),
                  pl.BlockSpec((None, d, tn), lambda l, j: (l, 0, j)),
                  pl.BlockSpec((None, 1, tn), lambda l, j: (l, 0, j))],
        out_specs=pl.BlockSpec((None, 8, tn), lambda l, j: (l, 0, j)),
        out_shape=jax.ShapeDtypeStruct((n_layers, 8, n6), F32),
        compiler_params=_cparams(("parallel", "parallel")),
        name="adaln_mod",
    )(cond, w_ada, b_ada.reshape(n_layers, 1, n6))


def _inproj_tile(a_scale, b_scale, xs, rows, mod, g_ref, wn_ref, wt_ref, tn_ref, tt_ref, pn_ref, pt_ref):
    def rotate(x1, x2, cs, sn):
        return (x1 * cs - x2 * sn).astype(BF16), (x1 * sn + x2 * cs).astype(BF16)

    ubs = []
    for x, rs in zip(xs, rows):
        u = _rms(x, 1e-6) * g_ref[...]
        ub = (u * (1.0 + mod[1:2]) + mod[0:1]).astype(BF16)
        ubs.append(ub)
        pn_ref[rs, 0:512] = _dot(ub, wn_ref[:, 0:512]).astype(BF16)
        ka = _dot(ub, wn_ref[:, 512:768])
        pn_ref[rs, 512:640], pn_ref[rs, 640:768] = rotate(ka[:, :LANES], ka[:, LANES:],
                                                          tn_ref[rs, 0:128], tn_ref[rs, 128:256])
        qb = _dot(ub, wn_ref[:, 768:1024])
        pn_ref[rs, 768:896], pn_ref[rs, 896:1024] = rotate(qb[:, :LANES], qb[:, LANES:],
                                                           tn_ref[rs, 256:384], tn_ref[rs, 384:512])
        pn_ref[rs, 1024:2048] = _dot(ub, wn_ref[:, 1024:2048]).astype(BF16)

    ub = jnp.concatenate(ubs, axis=0) if len(ubs) > 1 else ubs[0]

    def tr(r0, r1):
        return lax.dot_general(wt_ref[r0:r1, :], ub, (((1,), (1,)), ((), ())),
                               preferred_element_type=F32)

    qa = tr(0, 256) * a_scale
    pt_ref[0:128, :], pt_ref[128:256, :] = rotate(qa[:LANES], qa[LANES:], tt_ref[0:128, :], tt_ref[128:256, :])
    kb = tr(256, 512) * b_scale
    pt_ref[256:384, :], pt_ref[384:512, :] = rotate(kb[:LANES], kb[LANES:], tt_ref[256:384, :],
                                                    tt_ref[384:512, :])
    pt_ref[512:768, :] = tr(512, 768).astype(BF16)


def _inproj_kernel(a_scale, b_scale, h_ref, mod_ref, *refs):
    _inproj_tile(a_scale, b_scale, [h_ref[...]], [slice(None)], mod_ref[...], *refs)


def _inproj_specs(d, nn, ntr):
    return [pl.BlockSpec((1, d), lambda b, i: (0, 0)),
            pl.BlockSpec((d, nn), lambda b, i: (0, 0), pipeline_mode=pl.Buffered(1)),
            pl.BlockSpec((ntr, d), lambda b, i: (0, 0), pipeline_mode=pl.Buffered(1)),
            pl.BlockSpec((ROW_TILE, 4 * LANES), lambda b, i: (i, 0)),
            pl.BlockSpec((4 * LANES, ROW_TILE), lambda b, i: (0, i))]


def _inproj_outs(bsz, s, nn, ntr):
    return ([pl.BlockSpec((None, ROW_TILE, nn), lambda b, i: (b, i, 0)),
             pl.BlockSpec((None, ntr, ROW_TILE), lambda b, i: (b, 0, i))],
            [jax.ShapeDtypeStruct((bsz, s, nn), BF16), jax.ShapeDtypeStruct((bsz, ntr, s), BF16)])


def _inproj(h, mod, g_pre, w_nat, w_tr, tab_n, tab_t, nc, a_scale, b_scale):
    bsz, s, d = h.shape
    nt = s // ROW_TILE
    nn, ntr = w_nat.shape[1], w_tr.shape[0]
    msel = lambda b, i: (jnp.where(i < nc, bsz, b), 0, 0)
    out_specs, out_shape = _inproj_outs(bsz, s, nn, ntr)
    return pl.pallas_call(
        functools.partial(_inproj_kernel, a_scale, b_scale),
        grid=(bsz, nt),
        in_specs=[pl.BlockSpec((None, ROW_TILE, d), lambda b, i: (b, i, 0)),
                  pl.BlockSpec((None, 6, d), msel)] + _inproj_specs(d, nn, ntr),
        out_specs=out_specs,
        out_shape=out_shape,
        compiler_params=_cparams(("parallel", "parallel")),
        name="in_proj",
    )(h, mod, g_pre, w_nat, w_tr, tab_n, tab_t)


def _attn_kernel(nc, n_ksteps, lam_init, qt_ref, kc_ref, vtc_ref, kl_ref, vtl_ref, lamv_ref, g_ref,
                 o_ref, qz_ref, m_ref, acc_ref, tmax_ref, pv_ref, l_ref, lp_ref):
    qi = pl.program_id(1)
    ki = pl.program_id(2)
    n_groups = 2 * A_HEADS
    dv = vtc_ref.shape[0] // A_HEADS

    def attend(k_ref, vt_ref, skew):
        k = k_ref[...]
        s_q = [_dot(k, qz_ref[g]) for g in range(skew)]
        for g in range(n_groups):
            hd = g // 2
            s = s_q.pop(0)
            if g + skew < n_groups:
                s_q.append(_dot(k, qz_ref[g + skew]))
            m_old = m_ref[g]
            m_new = jnp.maximum(m_old, jnp.max(s, axis=0, keepdims=True))
            p = jnp.exp2(s - m_new)
            alpha = jnp.exp2(m_old - m_new)
            l_ref[g] = alpha * l_ref[g] + jnp.sum(p, axis=0, keepdims=True)
            acc_ref[g] = alpha * acc_ref[g] + _dot(vt_ref[hd * dv:(hd + 1) * dv, :], p.astype(BF16))
            m_ref[g] = m_new

    def attend_lagged(k_ref, vt_ref):
        k = k_ref[...]
        s_q = [_dot(k, qz_ref[g]) for g in range(ATTN_SKEW)]
        excess = None
        for g in range(n_groups):
            hd = g // 2
            s = s_q.pop(0)
            if g + ATTN_SKEW < n_groups:
                s_q.append(_dot(k, qz_ref[g + ATTN_SKEW]))
            m_fix = m_ref[g]
            p = jnp.exp2(s - m_fix)
            t_max = jnp.max(s, axis=0, keepdims=True)
            tmax_ref[g] = t_max
            lp_ref[g] = jnp.sum(p, axis=0, keepdims=True)
            excess = t_max - m_fix if excess is None else jnp.maximum(excess, t_max - m_fix)
            pv_ref[g] = _dot(vt_ref[hd * dv:(hd + 1) * dv, :], p.astype(BF16))
        return jnp.max(excess)

    def commit_lagged():
        for g in range(n_groups):
            m_old = m_ref[g]
            m_new = jnp.maximum(m_old, tmax_ref[g])
            alpha = jnp.exp2(m_old - m_new)
            acc_ref[g] = (acc_ref[g] + pv_ref[g]) * alpha
            l_ref[g] = (l_ref[g] + lp_ref[g]) * alpha
            m_ref[g] = m_new

    @pl.when(ki == 0)
    def _first():
        qt = qt_ref[...]
        rg = (lax.broadcasted_iota(jnp.int32, qt.shape, 0) & (LANES - 1)) >> 4
        for g in range(n_groups):
            qz_ref[g] = jnp.where(rg == g, qt, jnp.zeros_like(qt))
        m_ref[...] = jnp.full(m_ref.shape, -1e30, F32)
        acc_ref[...] = jnp.zeros(acc_ref.shape, F32)
        l_ref[...] = jnp.zeros(l_ref.shape, F32)
        attend(kc_ref, vtc_ref, n_groups)

    @pl.when((ki > 0) & (qi >= nc))
    def _latent():
        within = attend_lagged(kl_ref, vtl_ref) <= ATTN_HEADROOM

        @pl.when(within)
        def _commit():
            commit_lagged()

        @pl.when(jnp.logical_not(within))
        def _redo():
            attend(kl_ref, vtl_ref, ATTN_SKEW)

    @pl.when(ki == n_ksteps - 1)
    def _fin():
        lv = lamv_ref[...]
        a1 = jnp.sum(lv[0:1] * lv[1:2], axis=-1, keepdims=True)
        a2 = jnp.sum(lv[2:3] * lv[3:4], axis=-1, keepdims=True)
        lam = jnp.exp(a1) - jnp.exp(a2) + lam_init
        outs = []
        for hd in range(A_HEADS):
            o = acc_ref[2 * hd] / l_ref[2 * hd] - lam * (acc_ref[2 * hd + 1] / l_ref[2 * hd + 1])
            ms = jnp.mean(o * o, axis=0, keepdims=True)
            outs.append(o * lax.rsqrt(ms + 1e-5) * g_ref[...] * (1.0 - lam_init))
        o_ref[...] = jnp.concatenate(outs, axis=0).T.astype(o_ref.dtype)


def _attention(pn, pt, ctx_len, lamv, g_col, nc, lam_init):
    bsz, s, _ = pn.shape
    t_len = s - ctx_len
    nq = s // ROW_TILE
    tk = min(ATTN_KEY_TILE, t_len)
    assert t_len % tk == 0
    n_ksteps = 1 + t_len // tk
    w = MXU_TILE
    dv_ext = w // A_HEADS
    kl = lambda qi, ki: pl.multiple_of(ctx_len + jnp.where(qi < nc, 0, jnp.maximum(ki - 1, 0)) * tk, ROW_TILE)
    return pl.pallas_call(
        functools.partial(_attn_kernel, nc, n_ksteps, lam_init),
        grid=(bsz, nq, n_ksteps),
        in_specs=[pl.BlockSpec((None, w, ROW_TILE), lambda b, qi, ki: (b, 0, qi)),
                  pl.BlockSpec((None, ctx_len, w), lambda b, qi, ki: (b, 0, 2)),
                  pl.BlockSpec((None, w, ctx_len), lambda b, qi, ki: (b, 2, 0)),
                  pl.BlockSpec((None, pl.Element(tk), pl.Element(w)), lambda b, qi, ki: (b, kl(qi, ki), 2 * w)),
                  pl.BlockSpec((None, pl.Element(w), pl.Element(tk)), lambda b, qi, ki: (b, 2 * w, kl(qi, ki))),
                  pl.BlockSpec(lamv.shape, lambda b, qi, ki: (0, 0)),
                  pl.BlockSpec(g_col.shape, lambda b, qi, ki: (0, 0))],
        out_specs=pl.BlockSpec((None, ROW_TILE, w), lambda b, qi, ki: (b, qi, 0)),
        out_shape=jax.ShapeDtypeStruct((bsz, s, w), BF16),
        scratch_shapes=[pltpu.VMEM((2 * A_HEADS, w, ROW_TILE), BF16),
                        pltpu.VMEM((2 * A_HEADS, 1, ROW_TILE), F32),
                        pltpu.VMEM((2 * A_HEADS, dv_ext, ROW_TILE), F32),
                        pltpu.VMEM((2 * A_HEADS, 1, ROW_TILE), F32),
                        pltpu.VMEM((2 * A_HEADS, dv_ext, ROW_TILE), F32),
                        pltpu.VMEM((2 * A_HEADS, 1, ROW_TILE), F32),
                        pltpu.VMEM((2 * A_HEADS, 1, ROW_TILE), F32)],
        compiler_params=_cparams(("parallel", "parallel", "arbitrary")),
        name="diff_attn",
    )(pt, pn, pt, pn, pt, lamv, g_col)


def _group_norm(y, gmat, eps):
    yh = y.astype(BF16)
    yl = (y - yh.astype(F32)).astype(BF16)
    d = y - (_dot(yh, gmat) + _dot(yl, gmat))
    return d * lax.rsqrt(_dot((d * d).astype(BF16), gmat) + eps)


def _ret_direction(dr, q_ref, kt_ref, v_ref, y_ref, dm_ref, wend_ref, cross_ref, decs_ref, bd_ref,
                   gm_ref, s_ref):
    q = q_ref[...]
    kt = kt_ref[...]
    v = v_ref[...]
    rh = (lax.broadcasted_iota(jnp.int32, kt.shape, 0) & (LANES - 1)) >> 5
    ch = lax.broadcasted_iota(jnp.int32, (q.shape[0], v.shape[1]), 1) >> 6
    y = jnp.zeros((q.shape[0], v.shape[1]), F32)
    for hd in range(B_HEADS):
        ktz = jnp.where(rh == hd, kt, jnp.zeros_like(kt))
        p = (_dot(q, ktz) * dm_ref[dr, hd]).astype(BF16)
        y = jnp.where(ch == hd, _dot(p, v), y)
    s_old = s_ref[dr]
    y = y + _dot(q, s_old.astype(BF16)) * cross_ref[dr]
    kw = (kt.astype(F32) * wend_ref[dr]).astype(BF16)
    s_ref[dr] = decs_ref[...] * s_old + bd_ref[...] * _dot(kw, v)
    y_ref[...] = _group_norm(y, gm_ref[...], 1e-6).astype(y_ref.dtype)


def _fill_window_rows(ext_ref, shift_ref, prev_rows, rows, next_rows):
    r = rows.shape[0]
    ext_ref[0:HALO, :] = prev_rows
    ext_ref[HALO:HALO + r, :] = rows
    ext_ref[HALO + r:, :] = next_rows
    n = shift_ref.shape[1]
    for b in range(1, SUBLANES):
        shift_ref[b - 1] = ext_ref[b:b + n, :]


def _window_rows(ext_ref, shift_ref, o, n):
    a, b = divmod(o, SUBLANES)
    if b == 0:
        return ext_ref[o:o + n, :]
    return shift_ref[b - 1, SUBLANES * a:SUBLANES * a + n, :]


def _pool_mix(i, nc, seg_lens, halo_ok, pool_ref, pool_p, pool_n, wpool_ref, spool_ref, yc_ref, ext_ref,
              shift_ref):
    r, w = pool_ref.shape
    x = pool_ref[...].astype(F32)
    _fill_window_rows(ext_ref, shift_ref, pool_p[...].astype(F32) * halo_ok[0], x,
                      pool_n[...].astype(F32) * halo_ok[1])

    def sh(k):
        return _window_rows(ext_ref, shift_ref, HALO + k, r)

    w2 = sh(-1) + x
    w4 = w2 + sh(-2) + sh(1)
    w8 = w4 + sh(-4) + sh(-3) + sh(2) + sh(3)
    w16 = w8 + sh(-8) + sh(-7) + sh(-6) + sh(-5) + sh(4) + sh(5) + sh(6) + sh(7)
    grp = lax.broadcasted_iota(jnp.int32, (r, w), 1) >> 6
    wsum = jnp.where(grp == 0, w2, jnp.where(grp == 1, w4, jnp.where(grp == 2, w8, w16)))
    half = jnp.left_shift(1, grp)
    in_ctx = i < nc
    seg_len = jnp.where(in_ctx, seg_lens[0], seg_lens[1])
    pos = lax.broadcasted_iota(jnp.int32, (r, w), 0) + (i - jnp.where(in_ctx, 0, nc)) * r
    cnt = jnp.minimum(pos + half, seg_len) - jnp.maximum(pos - half, 0)
    dlt = wsum / cnt.astype(F32) - x
    yc_ref[...] = (_dot(dlt.astype(BF16), wpool_ref[...]) * spool_ref[...]).astype(yc_ref.dtype)


def _conv_mix(halo_ok, glu_ref, glu_p, glu_n, cw_ref, cb_ref, lng_ref, lnb_ref, wpw_ref, yd_ref, ext_ref,
              shift_ref):
    r = glu_ref.shape[0]
    w = ext_ref.shape[1]

    def glu(ref):
        v = ref[...].astype(F32)
        return v[:, :w] * jax.nn.sigmoid(v[:, w:])

    _fill_window_rows(ext_ref, shift_ref, glu(glu_p) * halo_ok[0], glu(glu_ref), glu(glu_n) * halo_ok[1])
    for r0 in range(0, r, CONV_ROWS):
        acc = jnp.zeros((CONV_ROWS, w), F32) + cb_ref[...]
        for k in range(CONV_K):
            acc = acc + _window_rows(ext_ref, shift_ref, HALO - CONV_K // 2 + k + r0,
                                     CONV_ROWS) * cw_ref[k:k + 1, :]
        mu = jnp.mean(acc, axis=-1, keepdims=True)
        d = acc - mu
        var = jnp.mean(d * d, axis=-1, keepdims=True)
        hn = d * lax.rsqrt(var + 1e-5) * lng_ref[...] + lnb_ref[...]
        yd_ref[r0:r0 + CONV_ROWS, :] = _dot(jax.nn.silu(hn).astype(BF16),
                                           wpw_ref[...]).astype(yd_ref.dtype)


def _bcd_kernel(nc, nt, seg_lens, qf_ref, ktf_ref, vf_ref, qb_ref, ktb_ref, vb_ref, dm_ref, wend_ref,
                cross_ref, decs_ref, bd_ref, gm_ref, glu_ref, glu_p, glu_n, pool_ref, pool_p, pool_n,
                wpool_ref, spool_ref, cw_ref, cb_ref, lng_ref, lnb_ref, wpw_ref,
                yf_ref, yb_ref, yc_ref, yd_ref, s_ref, ext_ref, shift_ref):
    i = pl.program_id(1)

    @pl.when(i == 0)
    def _init():
        s_ref[...] = jnp.zeros(s_ref.shape, F32)

    ret_tabs = (dm_ref, wend_ref, cross_ref, decs_ref, bd_ref, gm_ref, s_ref)
    halo_ok = (((i != 0) & (i != nc)).astype(F32), ((i != nc - 1) & (i != nt - 1)).astype(F32))
    _ret_direction(0, qf_ref, ktf_ref, vf_ref, yf_ref, *ret_tabs)
    _pool_mix(i, nc, seg_lens, halo_ok, pool_ref, pool_p, pool_n, wpool_ref, spool_ref, yc_ref, ext_ref,
              shift_ref)
    _ret_direction(1, qb_ref, ktb_ref, vb_ref, yb_ref, *ret_tabs)
    _conv_mix(halo_ok, glu_ref, glu_p, glu_n, cw_ref, cb_ref, lng_ref, lnb_ref, wpw_ref, yd_ref, ext_ref,
              shift_ref)


def _mixers_bcd(pn, pt, ret_tabs, cd_weights, nc, seg_lens):
    bsz, s, _ = pn.shape
    c = ROW_TILE
    nt = s // c
    w = MXU_TILE
    cb = lambda i: jnp.where(i < nc, nc - 1 - i, nt - 1 - (i - nc))
    hb = c // HALO
    prev = lambda i: jnp.maximum(i * hb - 1, 0)
    nxt = lambda i: jnp.minimum((i + 1) * hb, s // HALO - 1)
    const = lambda a: pl.BlockSpec(a.shape, lambda b, i: (0,) * a.ndim)
    tile = pl.BlockSpec((None, c, w), lambda b, i: (b, i, 0))
    sds = jax.ShapeDtypeStruct((bsz, s, w), BF16)
    return pl.pallas_call(
        functools.partial(_bcd_kernel, nc, nt, seg_lens),
        grid=(bsz, nt),
        in_specs=[pl.BlockSpec((None, c, w), lambda b, i: (b, i, 3)),
                  pl.BlockSpec((None, w, c), lambda b, i: (b, 1, i)),
                  pl.BlockSpec((None, c, w), lambda b, i: (b, i, 4)),
                  pl.BlockSpec((None, c, w), lambda b, i: (b, cb(i), 3)),
                  pl.BlockSpec((None, w, c), lambda b, i: (b, 1, cb(i))),
                  pl.BlockSpec((None, c, w), lambda b, i: (b, cb(i), 4))]
                 + [const(a) for a in ret_tabs]
                 + [pl.BlockSpec((None, c, 2 * w), lambda b, i: (b, i, 0)),
                    pl.BlockSpec((None, HALO, 2 * w), lambda b, i: (b, prev(i), 0)),
                    pl.BlockSpec((None, HALO, 2 * w), lambda b, i: (b, nxt(i), 0)),
                    pl.BlockSpec((None, c, w), lambda b, i: (b, i, 7)),
                    pl.BlockSpec((None, HALO, w), lambda b, i: (b, prev(i), 7)),
                    pl.BlockSpec((None, HALO, w), lambda b, i: (b, nxt(i), 7))]
                 + [const(a) for a in cd_weights],
        out_specs=[tile, pl.BlockSpec((None, c, w), lambda b, i: (b, cb(i), 0)), tile, tile],
        out_shape=[sds, sds, sds, sds],
        scratch_shapes=[pltpu.VMEM((2, w, w), F32),
                        pltpu.VMEM((c + 2 * HALO, w), F32),
                        pltpu.VMEM((SUBLANES - 1, c + 2 * HALO - SUBLANES, w), F32)],
        compiler_params=_cparams(("parallel", "arbitrary")),
        name="mixers_bcd",
    )(pn, pt, pn, pn, pt, pn, *ret_tabs, pn, pn, pn, pn, pn, pn, *cd_weights)


def _retention_tables(c):
    lg = jnp.asarray([math.log(1.0 - 2.0 ** (-5 - h)) for h in range(B_HEADS)], F32)
    j = jnp.arange(c, dtype=F32)
    dist = j[:, None] - j[None, :]
    df = jnp.where(dist >= 0, jnp.exp(lg[:, None, None] * jnp.maximum(dist, 0.0)), 0.0)
    dmask = jnp.stack([df, jnp.swapaxes(df, 1, 2)])
    row_head = (jnp.arange(MXU_TILE) % LANES) // 32
    col_head = jnp.arange(MXU_TILE) // 64
    wend_f = jnp.exp(lg[row_head][:, None] * (c - 1.0 - j)[None, :])
    wend_b = jnp.exp(lg[row_head][:, None] * j[None, :])
    cross_f = jnp.exp(lg[col_head][None, :] * (j + 1.0)[:, None])
    cross_b = jnp.exp(lg[col_head][None, :] * (c - j)[:, None])
    bdm = (row_head[:, None] == col_head[None, :]).astype(F32)
    decs = bdm * jnp.exp(lg * c)[col_head][None, :]
    gmat = ((col_head[:, None] == col_head[None, :]).astype(F32) / 64.0).astype(BF16)
    return (dmask, jnp.stack([wend_f, wend_b]), jnp.stack([cross_f, cross_b]), decs, bdm, gmat)


def _tail_kernel(next_scales, h_ref, mod_ref, gpm_ref, gprf_ref, gpof_ref, ya_ref, yf_ref, yb_ref, gf_ref,
                 gb_ref, yc_ref, yd_ref, wo_ref, wg_ref, wu_ref, wd_ref, *rest):
    o_ref = rest[0] if next_scales is None else rest[6]
    rb = h_ref.shape[0] // TAIL_SUBBLOCKS
    rows = [pl.ds(j * rb, rb) for j in range(TAIL_SUBBLOCKS)]

    def mix_in(rs):
        yb = (jax.nn.silu(gf_ref[rs, :].astype(F32)) * yf_ref[rs, :].astype(F32)
              + jax.nn.silu(gb_ref[rs, :].astype(F32)) * yb_ref[rs, :].astype(F32))
        ycat = jnp.concatenate([ya_ref[rs, :], yb.astype(BF16), yc_ref[rs, :], yd_ref[rs, :]], axis=-1)
        return _dot(ycat, wo_ref[...])

    ys = [mix_in(rs) for rs in rows]
    xs, us = [], []
    for rs, y in zip(rows, ys):
        x = h_ref[rs, :] + mod_ref[2:3, :] * (_rms(y, 1e-6) * gpm_ref[...])
        u = _rms(x, 1e-6) * gprf_ref[...]
        xs.append(x)
        us.append((u * (1.0 + mod_ref[4:5, :]) + mod_ref[3:4, :]).astype(BF16))
    acts = [(jax.nn.silu(_dot(ub, wg_ref[...])) * _dot(ub, wu_ref[...])).astype(BF16) for ub in us]
    fs = [_dot(a, wd_ref[...]) for a in acts]
    outs = [x + mod_ref[5:6, :] * (_rms(f, 1e-6) * gpof_ref[...]) for x, f in zip(xs, fs)]
    for rs, o in zip(rows, outs):
        o_ref[rs, :] = o
    if next_scales is not None:
        _inproj_tile(*next_scales, outs, rows, rest[0][...], *rest[1:6], *rest[7:9])


def _tail(h, mod, g_post_mix, g_pre_ffn, g_post_ffn, ya, yf, yb, pn, yc, yd, w_out, wg, wu, wd, nc,
          latent_only, next_inproj=None):
    bsz, s, d = h.shape
    r = ROW_TILE
    nt = s // r
    out_rows = s - nc * r if latent_only else s
    out_tile = (lambda b, i: (b, jnp.maximum(i - nc, 0), 0)) if latent_only else (lambda b, i: (b, i, 0))
    w = MXU_TILE
    msel = lambda b, i: (jnp.where(i < nc, bsz, b), 0, 0)
    tile = lambda col: pl.BlockSpec((None, r, w), lambda b, i: (b, i, col))
    vec = pl.BlockSpec((1, d), lambda b, i: (0, 0))
    resident = lambda a: pl.BlockSpec(a.shape, lambda b, i: (0, 0), pipeline_mode=pl.Buffered(1))
    in_specs = [pl.BlockSpec((None, r, d), lambda b, i: (b, i, 0)),
                pl.BlockSpec((None, 6, d), msel),
                vec, vec, vec,
                tile(0), tile(0), tile(0), tile(5), tile(6), tile(0), tile(0),
                resident(w_out), resident(wg), resident(wu), resident(wd)]
    args = [h, mod, g_post_mix, g_pre_ffn, g_post_ffn, ya, yf, yb, pn, pn, yc, yd, w_out, wg, wu, wd]
    out_specs = [pl.BlockSpec((None, r, d), out_tile)]
    out_shape = [jax.ShapeDtypeStruct((bsz, out_rows, d), F32)]
    next_scales = None
    if next_inproj is not None:
        next_scales = tuple(next_inproj[6:8])
        nn, ntr = next_inproj[2].shape[1], next_inproj[3].shape[0]
        in_specs += [pl.BlockSpec((None, 6, d), msel)] + _inproj_specs(d, nn, ntr)
        args += list(next_inproj[:6])
        pn_specs, pn_shapes = _inproj_outs(bsz, s, nn, ntr)
        out_specs += pn_specs
        out_shape += pn_shapes
    res = pl.pallas_call(
        functools.partial(_tail_kernel, next_scales),
        grid=(bsz, nt),
        in_specs=in_specs,
        out_specs=out_specs,
        out_shape=out_shape,
        compiler_params=_cparams(("parallel", "arbitrary")),
        name="out_ffn",
    )(*args)
    return res[0] if next_inproj is None else res


def _rope_tables(t_len, ctx_len, dqk_a, dk_b):
    rows = t_len // GRID_W
    row = jnp.repeat(jnp.arange(rows, dtype=F32), GRID_W)
    col = jnp.tile(jnp.arange(GRID_W, dtype=F32), rows)

    def cs(d, reps):
        n_freq = d // 4
        inv = ROPE_BASE ** (-jnp.arange(n_freq, dtype=F32) / n_freq)
        ang = jnp.concatenate([row[:, None] * inv, col[:, None] * inv], axis=-1)
        cos = jnp.concatenate([jnp.ones((ctx_len, d // 2), F32), jnp.cos(ang)], axis=0)
        sin = jnp.concatenate([jnp.zeros((ctx_len, d // 2), F32), jnp.sin(ang)], axis=0)
        return jnp.tile(cos, (1, reps)), jnp.tile(sin, (1, reps))

    ca, sa = cs(dqk_a, LANES // (dqk_a // 2))
    cb, sb = cs(dk_b, LANES // (dk_b // 2))
    tab_n = jnp.concatenate([ca, sa, cb, sb], axis=1)
    return tab_n, tab_n.T


def _rotary_perm(n_groups, d):
    half, g, i = jnp.meshgrid(jnp.arange(2), jnp.arange(n_groups), jnp.arange(d // 2), indexing="ij")
    return (g * d + half * (d // 2) + i).reshape(-1)


def kernel(x, c, ctx, c_ctx, w_ada, b_ada, g_pre_mix, g_post_mix, g_pre_ffn, g_post_ffn, w_in, w_out, lam_q1, lam_k1, lam_q2, lam_k2, g_subln, w_pool, s_pool, conv_w, conv_b, conv_ln_g, conv_ln_b, w_conv_out, w_ffn_gate, w_ffn_up, w_ffn_down):
    bsz, t_len, d = x.shape
    ctx_len = ctx.shape[1]
    depth = w_in.shape[0]
    assert d == 8 * A_HEADS * 32 and t_len % ROW_TILE == 0 and ctx_len % ROW_TILE == 0
    assert t_len % GRID_W == 0 and bsz < 8
    qw = d // 4
    dqk_a = qw // (2 * A_HEADS)
    dk_b = qw // B_HEADS
    nc = ctx_len // ROW_TILE

    h = jnp.concatenate([ctx, x], axis=1)

    cond = jnp.zeros((8, d), F32).at[:bsz].set(c).at[bsz].set(c_ctx)
    mods = _modulation(cond, w_ada, b_ada).reshape(depth, 8, 6, d)

    pa = _rotary_perm(2 * A_HEADS, dqk_a)
    pb = _rotary_perm(B_HEADS, dk_b)
    col = lambda k: w_in[:, :, k * qw:(k + 1) * qw]
    k_a, v_a, k_b, v_b, q_a, q_b, g_f, g_b, pool = (col(k) for k in range(9))
    glu = w_in[:, :, 9 * qw:]
    w_nat = jnp.concatenate([glu, k_a[:, :, pa], q_b[:, :, pb], v_b, g_f, g_b, pool], axis=-1).astype(BF16)
    w_tr = jnp.swapaxes(jnp.concatenate([q_a[:, :, pa], k_b[:, :, pb], v_a], axis=-1), 1, 2).astype(BF16)
    w_out_b = w_out.astype(BF16)
    wg_b, wu_b, wd_b = w_ffn_gate.astype(BF16), w_ffn_up.astype(BF16), w_ffn_down.astype(BF16)
    w_pw_b = w_conv_out.astype(BF16)
    eye = jnp.eye(len(POOL_WINDOWS), dtype=F32)
    wpool_bd = jnp.einsum("lgce,gh->lgche", w_pool, eye).reshape(depth, qw, qw).astype(BF16)
    conv_w2 = jnp.pad(conv_w.reshape(depth, CONV_K, qw), ((0, 0), (0, 1), (0, 0)))
    lamv = jnp.stack([lam_q1, lam_k1, lam_q2, lam_k2], axis=1)
    g_col = g_subln[:, :, None]

    tab_n, tab_t = _rope_tables(t_len, ctx_len, dqk_a, dk_b)
    ret_tabs = _retention_tables(ROW_TILE)
    row = lambda a, l: a[l][None, :]

    inproj_args = lambda l: (mods[l], row(g_pre_mix, l), w_nat[l], w_tr[l], tab_n, tab_t,
                             dqk_a ** -0.5 * LOG2_E, dk_b ** -0.5)
    m0, g0, wn0, wt0, _, _, a_scale, b_scale = inproj_args(0)
    pn, pt = _inproj(h, m0, g0, wn0, wt0, tab_n, tab_t, nc, a_scale, b_scale)
    for l in range(depth):
        lam_init = 0.8 - 0.6 * math.exp(-0.3 * l)
        mod = mods[l]
        ya = _attention(pn, pt, ctx_len, lamv[l], g_col[l], nc, lam_init)
        cd_weights = (wpool_bd[l], row(s_pool, l), conv_w2[l], row(conv_b, l), row(conv_ln_g, l),
                      row(conv_ln_b, l), w_pw_b[l])
        yf, yb, yc, yd = _mixers_bcd(pn, pt, ret_tabs, cd_weights, nc, (ctx_len, t_len))
        last = l == depth - 1
        res = _tail(h, mod, row(g_post_mix, l), row(g_pre_ffn, l), row(g_post_ffn, l), ya, yf, yb, pn,
                    yc, yd, w_out_b[l], wg_b[l], wu_b[l], wd_b[l], nc, latent_only=last,
                    next_inproj=None if last else inproj_args(l + 1))
        h, pn, pt = (res, None, None) if last else res
    return h
```

```python
import functools
import math

import jax
import jax.numpy as jnp
from jax import lax
from jax.experimental import pallas as pl
from jax.experimental.pallas import tpu as pltpu

F32 = jnp.float32
BF16 = jnp.bfloat16

GRID_W = 64
ROPE_BASE = 10000.0
A_HEADS = 4
B_HEADS = 4
POOL_WINDOWS = (2, 4, 8, 16)
CONV_K = 31
LOG2_E = math.log2(math.e)

LANES = 128
SUBLANES = 8
MXU_TILE = 256
ROW_TILE = 256
HALO = 16
VMEM_LIMIT = 56 * 1024 * 1024
TAIL_SUBBLOCKS = 2
ATTN_KEY_TILE = 4096
ATTN_SKEW = 3
ATTN_HEADROOM = 8.0
CONV_ROWS = 128


def _cparams(sem):
    return pltpu.CompilerParams(dimension_semantics=sem, vmem_limit_bytes=VMEM_LIMIT)


def _rms(x, eps):
    return x * lax.rsqrt(jnp.mean(x * x, axis=-1, keepdims=True) + eps)


def _dot(a, b):
    return jnp.dot(a, b, preferred_element_type=F32)


def _mod_kernel(c_ref, w_ref, b_ref, o_ref):
    a = jax.nn.silu(c_ref[...])
    o_ref[...] = jnp.dot(a, w_ref[...], preferred_element_type=F32,
                         precision=lax.Precision.HIGHEST) + b_ref[...]


def _modulation(cond, w_ada, b_ada):
    n_layers, d, n6 = w_ada.shape
    tn = 2048
    return pl.pallas_call(
        _mod_kernel,
        grid=(n_layers, n6 // tn),
        in_specs=[pl.BlockSpec((8, d), lambda l, j: (0, 0)),
                  pl.BlockSpec((None, d, tn), lambda l, j: (l, 0, j)),
                  pl.BlockSpec((None, 1, tn), lambda l, j: (l, 0, j))],
        out_specs=pl.BlockSpec((None, 8, tn), lambda l, j: (l, 0, j)),
        out_shape=jax.ShapeDtypeStruct((n_layers, 8, n6), F32),
        compiler_params=_cparams(("parallel", "parallel")),
        name="adaln_mod",
    )(cond, w_ada, b_ada.reshape(n_layers, 1, n6))


def _inproj_tile(a_scale, b_scale, xs, rows, mod, g_ref, wn_ref, wt_ref, tn_ref, tt_ref, pn_ref, pt_ref):
    def rotate(x1, x2, cs, sn):
        return (x1 * cs - x2 * sn).astype(BF16), (x1 * sn + x2 * cs).astype(BF16)

    ubs = []
    for x, rs in zip(xs, rows):
        u = _rms(x, 1e-6) * g_ref[...]
        ub = (u * (1.0 + mod[1:2]) + mod[0:1]).astype(BF16)
        ubs.append(ub)
        pn_ref[rs, 0:512] = _dot(ub, wn_ref[:, 0:512]).astype(BF16)
        ka = _dot(ub, wn_ref[:, 512:768])
        pn_ref[rs, 512:640], pn_ref[rs, 640:768] = rotate(ka[:, :LANES], ka[:, LANES:],
                                                          tn_ref[rs, 0:128], tn_ref[rs, 128:256])
        qb = _dot(ub, wn_ref[:, 768:1024])
        pn_ref[rs, 768:896], pn_ref[rs, 896:1024] = rotate(qb[:, :LANES], qb[:, LANES:],
                                                           tn_ref[rs, 256:384], tn_ref[rs, 384:512])
        pn_ref[rs, 1024:2048] = _dot(ub, wn_ref[:, 1024:2048]).astype(BF16)

    ub = jnp.concatenate(ubs, axis=0) if len(ubs) > 1 else ubs[0]

    def tr(r0, r1):
        return lax.dot_general(wt_ref[r0:r1, :], ub, (((1,), (1,)), ((), ())),
                               preferred_element_type=F32)

    qa = tr(0, 256) * a_scale
    pt_ref[0:128, :], pt_ref[128:256, :] = rotate(qa[:LANES], qa[LANES:], tt_ref[0:128, :], tt_ref[128:256, :])
    kb = tr(256, 512) * b_scale
    pt_ref[256:384, :], pt_ref[384:512, :] = rotate(kb[:LANES], kb[LANES:], tt_ref[256:384, :],
                                                    tt_ref[384:512, :])
    pt_ref[512:768, :] = tr(512, 768).astype(BF16)


def _inproj_kernel(a_scale, b_scale, nc, ctx_ref, x_ref, mod_ref, *refs):
    x = jnp.where(pl.program_id(1) < nc, ctx_ref[...], x_ref[...])
    refs[-1][...] = x
    _inproj_tile(a_scale, b_scale, [x], [slice(None)], mod_ref[...], *refs[:-1])


def _inproj_specs(d, nn, ntr):
    return [pl.BlockSpec((1, d), lambda b, i: (0, 0)),
            pl.BlockSpec((d, nn), lambda b, i: (0, 0), pipeline_mode=pl.Buffered(1)),
            pl.BlockSpec((ntr, d), lambda b, i: (0, 0), pipeline_mode=pl.Buffered(1)),
            pl.BlockSpec((ROW_TILE, 4 * LANES), lambda b, i: (i, 0)),
            pl.BlockSpec((4 * LANES, ROW_TILE), lambda b, i: (0, i))]


def _inproj_outs(bsz, s, nn, ntr):
    return ([pl.BlockSpec((None, ROW_TILE, nn), lambda b, i: (b, i, 0)),
             pl.BlockSpec((None, ntr, ROW_TILE), lambda b, i: (b, 0, i))],
            [jax.ShapeDtypeStruct((bsz, s, nn), BF16), jax.ShapeDtypeStruct((bsz, ntr, s), BF16)])


def _inproj_first(ctx, x, mod, g_pre, w_nat, w_tr, tab_n, tab_t, nc, a_scale, b_scale):
    bsz, t_len, d = x.shape
    s = ctx.shape[1] + t_len
    nt = s // ROW_TILE
    nn, ntr = w_nat.shape[1], w_tr.shape[0]
    msel = lambda b, i: (jnp.where(i < nc, bsz, b), 0, 0)
    out_specs, out_shape = _inproj_outs(bsz, s, nn, ntr)
    return pl.pallas_call(
        functools.partial(_inproj_kernel, a_scale, b_scale, nc),
        grid=(bsz, nt),
        in_specs=[pl.BlockSpec((None, ROW_TILE, d), lambda b, i: (b, jnp.minimum(i, nc - 1), 0)),
                  pl.BlockSpec((None, ROW_TILE, d), lambda b, i: (b, jnp.maximum(i - nc, 0), 0)),
                  pl.BlockSpec((None, 6, d), msel)] + _inproj_specs(d, nn, ntr),
        out_specs=out_specs + [pl.BlockSpec((None, ROW_TILE, d), lambda b, i: (b, i, 0))],
        out_shape=out_shape + [jax.ShapeDtypeStruct((bsz, s, d), F32)],
        compiler_params=_cparams(("parallel", "parallel")),
        name="in_proj",
    )(ctx, x, mod, g_pre, w_nat, w_tr, tab_n, tab_t)


def _attn_kernel(nc, n_ksteps, lam_init, qt_ref, kc_ref, vtc_ref, kl_ref, vtl_ref, lamv_ref, g_ref,
                 o_ref, qz_ref, m_ref, acc_ref, tmax_ref, pv_ref, l_ref, lp_ref):
    qi = pl.program_id(1)
    ki = pl.program_id(2)
    n_groups = 2 * A_HEADS
    dv = vtc_ref.shape[0] // A_HEADS

    def attend(k_ref, vt_ref, skew):
        k = k_ref[...]
        s_q = [_dot(k, qz_ref[g]) for g in range(skew)]
        for g in range(n_groups):
            hd = g // 2
            s = s_q.pop(0)
            if g + skew < n_groups:
                s_q.append(_dot(k, qz_ref[g + skew]))
            m_old = m_ref[g]
            m_new = jnp.maximum(m_old, jnp.max(s, axis=0, keepdims=True))
            p = jnp.exp2(s - m_new)
            alpha = jnp.exp2(m_old - m_new)
            l_ref[g] = alpha * l_ref[g] + jnp.sum(p, axis=0, keepdims=True)
            acc_ref[g] = alpha * acc_ref[g] + _dot(vt_ref[hd * dv:(hd + 1) * dv, :], p.astype(BF16))
            m_ref[g] = m_new

    def attend_lagged(k_ref, vt_ref):
        k = k_ref[...]
        s_q = [_dot(k, qz_ref[g]) for g in range(ATTN_SKEW)]
        excess = None
        for g in range(n_groups):
            hd = g // 2
            s = s_q.pop(0)
            if g + ATTN_SKEW < n_groups:
                s_q.append(_dot(k, qz_ref[g + ATTN_SKEW]))
            m_fix = m_ref[g]
            p = jnp.exp2(s - m_fix)
            t_max = jnp.max(s, axis=0, keepdims=True)
            tmax_ref[g] = t_max
            lp_ref[g] = jnp.sum(p, axis=0, keepdims=True)
            excess = t_max - m_fix if excess is None else jnp.maximum(excess, t_max - m_fix)
            pv_ref[g] = _dot(vt_ref[hd * dv:(hd + 1) * dv, :], p.astype(BF16))
        return jnp.max(excess)

    def commit_lagged():
        for g in range(n_groups):
            m_old = m_ref[g]
            m_new = jnp.maximum(m_old, tmax_ref[g])
            alpha = jnp.exp2(m_old - m_new)
            acc_ref[g] = (acc_ref[g] + pv_ref[g]) * alpha
            l_ref[g] = (l_ref[g] + lp_ref[g]) * alpha
            m_ref[g] = m_new

    @pl.when(ki == 0)
    def _first():
        qt = qt_ref[...]
        rg = (lax.broadcasted_iota(jnp.int32, qt.shape, 0) & (LANES - 1)) >> 4
        for g in range(n_groups):
            qz_ref[g] = jnp.where(rg == g, qt, jnp.zeros_like(qt))
        m_ref[...] = jnp.full(m_ref.shape, -1e30, F32)
        acc_ref[...] = jnp.zeros(acc_ref.shape, F32)
        l_ref[...] = jnp.zeros(l_ref.shape, F32)
        attend(kc_ref, vtc_ref, n_groups)

    @pl.when((ki > 0) & (qi >= nc))
    def _latent():
        within = attend_lagged(kl_ref, vtl_ref) <= ATTN_HEADROOM

        @pl.when(within)
        def _commit():
            commit_lagged()

        @pl.when(jnp.logical_not(within))
        def _redo():
            attend(kl_ref, vtl_ref, ATTN_SKEW)

    @pl.when(ki == n_ksteps - 1)
    def _fin():
        lv = lamv_ref[...]
        a1 = jnp.sum(lv[0:1] * lv[1:2], axis=-1, keepdims=True)
        a2 = jnp.sum(lv[2:3] * lv[3:4], axis=-1, keepdims=True)
        lam = jnp.exp(a1) - jnp.exp(a2) + lam_init
        outs = []
        for hd in range(A_HEADS):
            o = acc_ref[2 * hd] / l_ref[2 * hd] - lam * (acc_ref[2 * hd + 1] / l_ref[2 * hd + 1])
            ms = jnp.mean(o * o, axis=0, keepdims=True)
            outs.append(o * lax.rsqrt(ms + 1e-5) * g_ref[...] * (1.0 - lam_init))
        o_ref[...] = jnp.concatenate(outs, axis=0).T.astype(o_ref.dtype)


def _attention(pn, pt, ctx_len, lamv, g_col, nc, lam_init):
    bsz, s, _ = pn.shape
    t_len = s - ctx_len
    nq = s // ROW_TILE
    tk = min(ATTN_KEY_TILE, t_len)
    assert t_len % tk == 0
    n_ksteps = 1 + t_len // tk
    w = MXU_TILE
    dv_ext = w // A_HEADS
    kl = lambda qi, ki: pl.multiple_of(ctx_len + jnp.where(qi < nc, 0, jnp.maximum(ki - 1, 0)) * tk, ROW_TILE)
    return pl.pallas_call(
        functools.partial(_attn_kernel, nc, n_ksteps, lam_init),
        grid=(bsz, nq, n_ksteps),
        in_specs=[pl.BlockSpec((None, w, ROW_TILE), lambda b, qi, ki: (b, 0, qi)),
                  pl.BlockSpec((None, ctx_len, w), lambda b, qi, ki: (b, 0, 2)),
                  pl.BlockSpec((None, w, ctx_len), lambda b, qi, ki: (b, 2, 0)),
                  pl.BlockSpec((None, pl.Element(tk), pl.Element(w)), lambda b, qi, ki: (b, kl(qi, ki), 2 * w)),
                  pl.BlockSpec((None, pl.Element(w), pl.Element(tk)), lambda b, qi, ki: (b, 2 * w, kl(qi, ki))),
                  pl.BlockSpec(lamv.shape, lambda b, qi, ki: (0, 0)),
                  pl.BlockSpec(g_col.shape, lambda b, qi, ki: (0, 0))],
        out_specs=pl.BlockSpec((None, ROW_TILE, w), lambda b, qi, ki: (b, qi, 0)),
        out_shape=jax.ShapeDtypeStruct((bsz, s, w), BF16),
        scratch_shapes=[pltpu.VMEM((2 * A_HEADS, w, ROW_TILE), BF16),
                        pltpu.VMEM((2 * A_HEADS, 1, ROW_TILE), F32),
                        pltpu.VMEM((2 * A_HEADS, dv_ext, ROW_TILE), F32),
                        pltpu.VMEM((2 * A_HEADS, 1, ROW_TILE), F32),
                        pltpu.VMEM((2 * A_HEADS, dv_ext, ROW_TILE), F32),
                        pltpu.VMEM((2 * A_HEADS, 1, ROW_TILE), F32),
                        pltpu.VMEM((2 * A_HEADS, 1, ROW_TILE), F32)],
        compiler_params=_cparams(("parallel", "parallel", "arbitrary")),
        name="diff_attn",
    )(pt, pn, pt, pn, pt, lamv, g_col)


def _group_norm(y, gmat, eps):
    yh = y.astype(BF16)
    yl = (y - yh.astype(F32)).astype(BF16)
    d = y - (_dot(yh, gmat) + _dot(yl, gmat))
    return d * lax.rsqrt(_dot((d * d).astype(BF16), gmat) + eps)


def _ret_direction(dr, q_ref, kt_ref, v_ref, y_ref, dm_ref, wend_ref, cross_ref, decs_ref, bd_ref,
                   gm_ref, s_ref):
    q = q_ref[...]
    kt = kt_ref[...]
    v = v_ref[...]
    rh = (lax.broadcasted_iota(jnp.int32, kt.shape, 0) & (LANES - 1)) >> 5
    ch = lax.broadcasted_iota(jnp.int32, (q.shape[0], v.shape[1]), 1) >> 6
    y = jnp.zeros((q.shape[0], v.shape[1]), F32)
    for hd in range(B_HEADS):
        ktz = jnp.where(rh == hd, kt, jnp.zeros_like(kt))
        p = (_dot(q, ktz) * dm_ref[dr, hd]).astype(BF16)
        y = jnp.where(ch == hd, _dot(p, v), y)
    s_old = s_ref[dr]
    y = y + _dot(q, s_old.astype(BF16)) * cross_ref[dr]
    kw = (kt.astype(F32) * wend_ref[dr]).astype(BF16)
    s_ref[dr] = decs_ref[...] * s_old + bd_ref[...] * _dot(kw, v)
    y_ref[...] = _group_norm(y, gm_ref[...], 1e-6).astype(y_ref.dtype)


def _fill_window_rows(ext_ref, shift_ref, prev_rows, rows, next_rows):
    r = rows.shape[0]
    ext_ref[0:HALO, :] = prev_rows
    ext_ref[HALO:HALO + r, :] = rows
    ext_ref[HALO + r:, :] = next_rows
    n = shift_ref.shape[1]
    for b in range(1, SUBLANES):
        shift_ref[b - 1] = ext_ref[b:b + n, :]


def _window_rows(ext_ref, shift_ref, o, n):
    a, b = divmod(o, SUBLANES)
    if b == 0:
        return ext_ref[o:o + n, :]
    return shift_ref[b - 1, SUBLANES * a:SUBLANES * a + n, :]


def _pool_mix(i, nc, seg_lens, halo_ok, pool_ref, pool_p, pool_n, wpool_ref, spool_ref, yc_ref, ext_ref,
              shift_ref):
    r, w = pool_ref.shape
    x = pool_ref[...].astype(F32)
    _fill_window_rows(ext_ref, shift_ref, pool_p[...].astype(F32) * halo_ok[0], x,
                      pool_n[...].astype(F32) * halo_ok[1])

    def sh(k):
        return _window_rows(ext_ref, shift_ref, HALO + k, r)

    w2 = sh(-1) + x
    w4 = w2 + sh(-2) + sh(1)
    w8 = w4 + sh(-4) + sh(-3) + sh(2) + sh(3)
    w16 = w8 + sh(-8) + sh(-7) + sh(-6) + sh(-5) + sh(4) + sh(5) + sh(6) + sh(7)
    grp = lax.broadcasted_iota(jnp.int32, (r, w), 1) >> 6
    wsum = jnp.where(grp == 0, w2, jnp.where(grp == 1, w4, jnp.where(grp == 2, w8, w16)))
    half = jnp.left_shift(1, grp)
    in_ctx = i < nc
    seg_len = jnp.where(in_ctx, seg_lens[0], seg_lens[1])
    pos = lax.broadcasted_iota(jnp.int32, (r, w), 0) + (i - jnp.where(in_ctx, 0, nc)) * r
    cnt = jnp.minimum(pos + half, seg_len) - jnp.maximum(pos - half, 0)
    dlt = wsum / cnt.astype(F32) - x
    yc_ref[...] = (_dot(dlt.astype(BF16), wpool_ref[...]) * spool_ref[...]).astype(yc_ref.dtype)


def _conv_mix(halo_ok, glu_ref, glu_p, glu_n, cw_ref, cb_ref, lng_ref, lnb_ref, wpw_ref, yd_ref, ext_ref,
              shift_ref):
    r = glu_ref.shape[0]
    w = ext_ref.shape[1]

    def glu(ref):
        v = ref[...].astype(F32)
        return v[:, :w] * jax.nn.sigmoid(v[:, w:])

    _fill_window_rows(ext_ref, shift_ref, glu(glu_p) * halo_ok[0], glu(glu_ref), glu(glu_n) * halo_ok[1])
    for r0 in range(0, r, CONV_ROWS):
        acc = jnp.zeros((CONV_ROWS, w), F32) + cb_ref[...]
        for k in range(CONV_K):
            acc = acc + _window_rows(ext_ref, shift_ref, HALO - CONV_K // 2 + k + r0,
                                     CONV_ROWS) * cw_ref[k:k + 1, :]
        mu = jnp.mean(acc, axis=-1, keepdims=True)
        d = acc - mu
        var = jnp.mean(d * d, axis=-1, keepdims=True)
        hn = d * lax.rsqrt(var + 1e-5) * lng_ref[...] + lnb_ref[...]
        yd_ref[r0:r0 + CONV_ROWS, :] = _dot(jax.nn.silu(hn).astype(BF16),
                                           wpw_ref[...]).astype(yd_ref.dtype)


def _bcd_kernel(nc, nt, seg_lens, qf_ref, ktf_ref, vf_ref, qb_ref, ktb_ref, vb_ref, dm_ref, wend_ref,
                cross_ref, decs_ref, bd_ref, gm_ref, glu_ref, glu_p, glu_n, pool_ref, pool_p, pool_n,
                wpool_ref, spool_ref, cw_ref, cb_ref, lng_ref, lnb_ref, wpw_ref,
                yf_ref, yb_ref, yc_ref, yd_ref, s_ref, ext_ref, shift_ref):
    i = pl.program_id(1)

    @pl.when(i == 0)
    def _init():
        s_ref[...] = jnp.zeros(s_ref.shape, F32)

    ret_tabs = (dm_ref, wend_ref, cross_ref, decs_ref, bd_ref, gm_ref, s_ref)
    halo_ok = (((i != 0) & (i != nc)).astype(F32), ((i != nc - 1) & (i != nt - 1)).astype(F32))
    _ret_direction(0, qf_ref, ktf_ref, vf_ref, yf_ref, *ret_tabs)
    _pool_mix(i, nc, seg_lens, halo_ok, pool_ref, pool_p, pool_n, wpool_ref, spool_ref, yc_ref, ext_ref,
              shift_ref)
    _ret_direction(1, qb_ref, ktb_ref, vb_ref, yb_ref, *ret_tabs)
    _conv_mix(halo_ok, glu_ref, glu_p, glu_n, cw_ref, cb_ref, lng_ref, lnb_ref, wpw_ref, yd_ref, ext_ref,
              shift_ref)


def _mixers_bcd(pn, pt, ret_tabs, cd_weights, nc, seg_lens):
    bsz, s, _ = pn.shape
    c = ROW_TILE
    nt = s // c
    w = MXU_TILE
    cb = lambda i: jnp.where(i < nc, nc - 1 - i, nt - 1 - (i - nc))
    hb = c // HALO
    prev = lambda i: jnp.maximum(i * hb - 1, 0)
    nxt = lambda i: jnp.minimum((i + 1) * hb, s // HALO - 1)
    const = lambda a: pl.BlockSpec(a.shape, lambda b, i: (0,) * a.ndim)
    tile = pl.BlockSpec((None, c, w), lambda b, i: (b, i, 0))
    sds = jax.ShapeDtypeStruct((bsz, s, w), BF16)
    return pl.pallas_call(
        functools.partial(_bcd_kernel, nc, nt, seg_lens),
        grid=(bsz, nt),
        in_specs=[pl.BlockSpec((None, c, w), lambda b, i: (b, i, 3)),
                  pl.BlockSpec((None, w, c), lambda b, i: (b, 1, i)),
                  pl.BlockSpec((None, c, w), lambda b, i: (b, i, 4)),
                  pl.BlockSpec((None, c, w), lambda b, i: (b, cb(i), 3)),
                  pl.BlockSpec((None, w, c), lambda b, i: (b, 1, cb(i))),
                  pl.BlockSpec((None, c, w), lambda b, i: (b, cb(i), 4))]
                 + [const(a) for a in ret_tabs]
                 + [pl.BlockSpec((None, c, 2 * w), lambda b, i: (b, i, 0)),
                    pl.BlockSpec((None, HALO, 2 * w), lambda b, i: (b, prev(i), 0)),
                    pl.BlockSpec((None, HALO, 2 * w), lambda b, i: (b, nxt(i), 0)),
                    pl.BlockSpec((None, c, w), lambda b, i: (b, i, 7)),
                    pl.BlockSpec((None, HALO, w), lambda b, i: (b, prev(i), 7)),
                    pl.BlockSpec((None, HALO, w), lambda b, i: (b, nxt(i), 7))]
                 + [const(a) for a in cd_weights],
        out_specs=[tile, pl.BlockSpec((None, c, w), lambda b, i: (b, cb(i), 0)), tile, tile],
        out_shape=[sds, sds, sds, sds],
        scratch_shapes=[pltpu.VMEM((2, w, w), F32),
                        pltpu.VMEM((c + 2 * HALO, w), F32),
                        pltpu.VMEM((SUBLANES - 1, c + 2 * HALO - SUBLANES, w), F32)],
        compiler_params=_cparams(("parallel", "arbitrary")),
        name="mixers_bcd",
    )(pn, pt, pn, pn, pt, pn, *ret_tabs, pn, pn, pn, pn, pn, pn, *cd_weights)


def _retention_tables(c):
    lg = jnp.asarray([math.log(1.0 - 2.0 ** (-5 - h)) for h in range(B_HEADS)], F32)
    j = jnp.arange(c, dtype=F32)
    dist = j[:, None] - j[None, :]
    df = jnp.where(dist >= 0, jnp.exp(lg[:, None, None] * jnp.maximum(dist, 0.0)), 0.0)
    dmask = jnp.stack([df, jnp.swapaxes(df, 1, 2)])
    row_head = (jnp.arange(MXU_TILE) % LANES) // 32
    col_head = jnp.arange(MXU_TILE) // 64
    wend_f = jnp.exp(lg[row_head][:, None] * (c - 1.0 - j)[None, :])
    wend_b = jnp.exp(lg[row_head][:, None] * j[None, :])
    cross_f = jnp.exp(lg[col_head][None, :] * (j + 1.0)[:, None])
    cross_b = jnp.exp(lg[col_head][None, :] * (c - j)[:, None])
    bdm = (row_head[:, None] == col_head[None, :]).astype(F32)
    decs = bdm * jnp.exp(lg * c)[col_head][None, :]
    gmat = ((col_head[:, None] == col_head[None, :]).astype(F32) / 64.0).astype(BF16)
    return (dmask, jnp.stack([wend_f, wend_b]), jnp.stack([cross_f, cross_b]), decs, bdm, gmat)


def _tail_kernel(next_scales, h_ref, mod_ref, gpm_ref, gprf_ref, gpof_ref, ya_ref, yf_ref, yb_ref, gf_ref,
                 gb_ref, yc_ref, yd_ref, wo_ref, wg_ref, wu_ref, wd_ref, *rest):
    o_ref = rest[0] if next_scales is None else rest[6]
    rb = h_ref.shape[0] // TAIL_SUBBLOCKS
    rows = [pl.ds(j * rb, rb) for j in range(TAIL_SUBBLOCKS)]

    def mix_in(rs):
        yb = (jax.nn.silu(gf_ref[rs, :].astype(F32)) * yf_ref[rs, :].astype(F32)
              + jax.nn.silu(gb_ref[rs, :].astype(F32)) * yb_ref[rs, :].astype(F32))
        ycat = jnp.concatenate([ya_ref[rs, :], yb.astype(BF16), yc_ref[rs, :], yd_ref[rs, :]], axis=-1)
        return _dot(ycat, wo_ref[...])

    ys = [mix_in(rs) for rs in rows]
    xs, us = [], []
    for rs, y in zip(rows, ys):
        x = h_ref[rs, :] + mod_ref[2:3, :] * (_rms(y, 1e-6) * gpm_ref[...])
        u = _rms(x, 1e-6) * gprf_ref[...]
        xs.append(x)
        us.append((u * (1.0 + mod_ref[4:5, :]) + mod_ref[3:4, :]).astype(BF16))
    acts = [(jax.nn.silu(_dot(ub, wg_ref[...])) * _dot(ub, wu_ref[...])).astype(BF16) for ub in us]
    fs = [_dot(a, wd_ref[...]) for a in acts]
    outs = [x + mod_ref[5:6, :] * (_rms(f, 1e-6) * gpof_ref[...]) for x, f in zip(xs, fs)]
    for rs, o in zip(rows, outs):
        o_ref[rs, :] = o
    if next_scales is not None:
        _inproj_tile(*next_scales, outs, rows, rest[0][...], *rest[1:6], *rest[7:9])


def _tail(h, mod, g_post_mix, g_pre_ffn, g_post_ffn, ya, yf, yb, pn, yc, yd, w_out, wg, wu, wd, nc,
          latent_only, next_inproj=None):
    bsz, s, d = h.shape
    r = ROW_TILE
    nt = s // r
    out_rows = s - nc * r if latent_only else s
    out_tile = (lambda b, i: (b, jnp.maximum(i - nc, 0), 0)) if latent_only else (lambda b, i: (b, i, 0))
    w = MXU_TILE
    msel = lambda b, i: (jnp.where(i < nc, bsz, b), 0, 0)
    tile = lambda col: pl.BlockSpec((None, r, w), lambda b, i: (b, i, col))
    vec = pl.BlockSpec((1, d), lambda b, i: (0, 0))
    resident = lambda a: pl.BlockSpec(a.shape, lambda b, i: (0, 0), pipeline_mode=pl.Buffered(1))
    in_specs = [pl.BlockSpec((None, r, d), lambda b, i: (b, i, 0)),
                pl.BlockSpec((None, 6, d), msel),
                vec, vec, vec,
                tile(0), tile(0), tile(0), tile(5), tile(6), tile(0), tile(0),
                resident(w_out), resident(wg), resident(wu), resident(wd)]
    args = [h, mod, g_post_mix, g_pre_ffn, g_post_ffn, ya, yf, yb, pn, pn, yc, yd, w_out, wg, wu, wd]
    out_specs = [pl.BlockSpec((None, r, d), out_tile)]
    out_shape = [jax.ShapeDtypeStruct((bsz, out_rows, d), F32)]
    next_scales = None
    if next_inproj is not None:
        next_scales = tuple(next_inproj[6:8])
        nn, ntr = next_inproj[2].shape[1], next_inproj[3].shape[0]
        in_specs += [pl.BlockSpec((None, 6, d), msel)] + _inproj_specs(d, nn, ntr)
        args += list(next_inproj[:6])
        pn_specs, pn_shapes = _inproj_outs(bsz, s, nn, ntr)
        out_specs += pn_specs
        out_shape += pn_shapes
    res = pl.pallas_call(
        functools.partial(_tail_kernel, next_scales),
        grid=(bsz, nt),
        in_specs=in_specs,
        out_specs=out_specs,
        out_shape=out_shape,
        compiler_params=_cparams(("parallel", "arbitrary")),
        name="out_ffn",
    )(*args)
    return res[0] if next_inproj is None else res


def _rope_tables(t_len, ctx_len, dqk_a, dk_b):
    rows = t_len // GRID_W
    row = jnp.repeat(jnp.arange(rows, dtype=F32), GRID_W)
    col = jnp.tile(jnp.arange(GRID_W, dtype=F32), rows)

    def cs(d, reps):
        n_freq = d // 4
        inv = ROPE_BASE ** (-jnp.arange(n_freq, dtype=F32) / n_freq)
        ang = jnp.concatenate([row[:, None] * inv, col[:, None] * inv], axis=-1)
        cos = jnp.concatenate([jnp.ones((ctx_len, d // 2), F32), jnp.cos(ang)], axis=0)
        sin = jnp.concatenate([jnp.zeros((ctx_len, d // 2), F32), jnp.sin(ang)], axis=0)
        return jnp.tile(cos, (1, reps)), jnp.tile(sin, (1, reps))

    ca, sa = cs(dqk_a, LANES // (dqk_a // 2))
    cb, sb = cs(dk_b, LANES // (dk_b // 2))
    tab_n = jnp.concatenate([ca, sa, cb, sb], axis=1)
    return tab_n, tab_n.T


def _split_rotary_halves(w, n_groups):
    lead, n = w.shape[:-1], w.shape[-1]
    return jnp.swapaxes(w.reshape(*lead, n_groups, 2, n // (2 * n_groups)), -3, -2).reshape(*lead, n)


def kernel(x, c, ctx, c_ctx, w_ada, b_ada, g_pre_mix, g_post_mix, g_pre_ffn, g_post_ffn, w_in, w_out, lam_q1, lam_k1, lam_q2, lam_k2, g_subln, w_pool, s_pool, conv_w, conv_b, conv_ln_g, conv_ln_b, w_conv_out, w_ffn_gate, w_ffn_up, w_ffn_down):
    bsz, t_len, d = x.shape
    ctx_len = ctx.shape[1]
    depth = w_in.shape[0]
    assert d == 8 * A_HEADS * 32 and t_len % ROW_TILE == 0 and ctx_len % ROW_TILE == 0
    assert t_len % GRID_W == 0 and bsz < 8
    qw = d // 4
    dqk_a = qw // (2 * A_HEADS)
    dk_b = qw // B_HEADS
    nc = ctx_len // ROW_TILE

    cond = jnp.zeros((8, d), F32).at[:bsz].set(c).at[bsz].set(c_ctx)
    mods = _modulation(cond, w_ada, b_ada).reshape(depth, 8, 6, d)

    col = lambda k: w_in[:, :, k * qw:(k + 1) * qw]
    k_a, v_a, k_b, v_b, q_a, q_b, g_f, g_b, pool = (col(k) for k in range(9))
    glu = w_in[:, :, 9 * qw:]
    w_nat = jnp.concatenate([glu, _split_rotary_halves(k_a, 2 * A_HEADS), _split_rotary_halves(q_b, B_HEADS), v_b, g_f, g_b, pool], axis=-1).astype(BF16)
    w_tr = jnp.swapaxes(jnp.concatenate([_split_rotary_halves(q_a, 2 * A_HEADS), _split_rotary_halves(k_b, B_HEADS), v_a], axis=-1), 1, 2).astype(BF16)
    w_out_b = w_out.astype(BF16)
    wg_b, wu_b, wd_b = w_ffn_gate.astype(BF16), w_ffn_up.astype(BF16), w_ffn_down.astype(BF16)
    w_pw_b = w_conv_out.astype(BF16)
    eye = jnp.eye(len(POOL_WINDOWS), dtype=F32)
    wpool_bd = jnp.einsum("lgce,gh->lgche", w_pool, eye).reshape(depth, qw, qw).astype(BF16)
    conv_w2 = jnp.pad(conv_w.reshape(depth, CONV_K, qw), ((0, 0), (0, 1), (0, 0)))
    lamv = jnp.stack([lam_q1, lam_k1, lam_q2, lam_k2], axis=1)
    g_col = g_subln[:, :, None]

    tab_n, tab_t = _rope_tables(t_len, ctx_len, dqk_a, dk_b)
    ret_tabs = _retention_tables(ROW_TILE)
    row = lambda a, l: a[l][None, :]

    inproj_args = lambda l: (mods[l], row(g_pre_mix, l), w_nat[l], w_tr[l], tab_n, tab_t,
                             dqk_a ** -0.5 * LOG2_E, dk_b ** -0.5)
    m0, g0, wn0, wt0, _, _, a_scale, b_scale = inproj_args(0)
    pn, pt, h = _inproj_first(ctx, x, m0, g0, wn0, wt0, tab_n, tab_t, nc, a_scale, b_scale)
    for l in range(depth):
        lam_init = 0.8 - 0.6 * math.exp(-0.3 * l)
        mod = mods[l]
        ya = _attention(pn, pt, ctx_len, lamv[l], g_col[l], nc, lam_init)
        cd_weights = (wpool_bd[l], row(s_pool, l), conv_w2[l], row(conv_b, l), row(conv_ln_g, l),
                      row(conv_ln_b, l), w_pw_b[l])
        yf, yb, yc, yd = _mixers_bcd(pn, pt, ret_tabs, cd_weights, nc, (ctx_len, t_len))
        last = l == depth - 1
        res = _tail(h, mod, row(g_post_mix, l), row(g_pre_ffn, l), row(g_post_ffn, l), ya, yf, yb, pn,
                    yc, yd, w_out_b[l], wg_b[l], wu_b[l], wd_b[l], nc, latent_only=last,
                    next_inproj=None if last else inproj_args(l + 1))
        h, pn, pt = (res, None, None) if last else res
    return h
```

```python
import functools
import math

import jax
import jax.numpy as jnp
from jax import lax
from jax.experimental import pallas as pl
from jax.experimental.pallas import tpu as pltpu

F32 = jnp.float32
BF16 = jnp.bfloat16

GRID_W = 64
ROPE_BASE = 10000.0
A_HEADS = 4
B_HEADS = 4
POOL_WINDOWS = (2, 4, 8, 16)
CONV_K = 31
LOG2_E = math.log2(math.e)

LANES = 128
SUBLANES = 8
MXU_TILE = 256
ROW_TILE = 256
HALO = 16
VMEM_LIMIT = 56 * 1024 * 1024
TAIL_SUBBLOCKS = 2
ATTN_KEY_TILE = 4096
ATTN_SKEW = 3
ATTN_HEADROOM = 8.0
CONV_ROWS = 128


def _cparams(sem):
    return pltpu.CompilerParams(dimension_semantics=sem, vmem_limit_bytes=VMEM_LIMIT)


def _rms(x, eps):
    return x * lax.rsqrt(jnp.mean(x * x, axis=-1, keepdims=True) + eps)


def _dot(a, b):
    return jnp.dot(a, b, preferred_element_type=F32)


def _mod_kernel(c_ref, w_ref, b_ref, o_ref):
    a = jax.nn.silu(c_ref[...])
    o_ref[...] = jnp.dot(a, w_ref[...], preferred_element_type=F32,
                         precision=lax.Precision.HIGHEST) + b_ref[...]


def _modulation(cond, w_ada, b_ada):
    n_layers, d, n6 = w_ada.shape
    tn = 2048
    return pl.pallas_call(
        _mod_kernel,
        grid=(n_layers, n6 // tn),
        in_specs=[pl.BlockSpec((8, d), lambda l, j: (0, 0)),
                  pl.BlockSpec((None, d, tn), lambda l, j: (l, 0, j)),
                  pl.BlockSpec((None, 1, tn), lambda l, j: (l, 0, j))],
        out_specs=pl.BlockSpec((None, 8, tn), lambda l, j: (l, 0, j)),
        out_shape=jax.ShapeDtypeStruct((n_layers, 8, n6), F32),
        compiler_params=_cparams(("parallel", "parallel")),
        name="adaln_mod",
    )(cond, w_ada, b_ada.reshape(n_layers, 1, n6))


def _inproj_tile(a_scale, b_scale, xs, rows, mod, g_ref, wn_ref, wt_ref, tn_ref, tt_ref, pn_ref, pt_ref):
    def rotate(x1, x2, cs, sn):
        return (x1 * cs - x2 * sn).astype(BF16), (x1 * sn + x2 * cs).astype(BF16)

    ubs = []
    for x, rs in zip(xs, rows):
        u = _rms(x, 1e-6) * g_ref[...]
        ub = (u * (1.0 + mod[1:2]) + mod[0:1]).astype(BF16)
        ubs.append(ub)
        pn_ref[rs, 0:512] = _dot(ub, wn_ref[:, 0:512]).astype(BF16)
        ka = _dot(ub, wn_ref[:, 512:768])
        pn_ref[rs, 512:640], pn_ref[rs, 640:768] = rotate(ka[:, :LANES], ka[:, LANES:],
                                                          tn_ref[rs, 0:128], tn_ref[rs, 128:256])
        qb = _dot(ub, wn_ref[:, 768:1024])
        pn_ref[rs, 768:896], pn_ref[rs, 896:1024] = rotate(qb[:, :LANES], qb[:, LANES:],
                                                           tn_ref[rs, 256:384], tn_ref[rs, 384:512])
        pn_ref[rs, 1024:2048] = _dot(ub, wn_ref[:, 1024:2048]).astype(BF16)

    ub = jnp.concatenate(ubs, axis=0) if len(ubs) > 1 else ubs[0]

    def tr(r0, r1):
        return lax.dot_general(wt_ref[r0:r1, :], ub, (((1,), (1,)), ((), ())),
                               preferred_element_type=F32)

    qa = tr(0, 256) * a_scale
    pt_ref[0:128, :], pt_ref[128:256, :] = rotate(qa[:LANES], qa[LANES:], tt_ref[0:128, :], tt_ref[128:256, :])
    kb = tr(256, 512) * b_scale
    pt_ref[256:384, :], pt_ref[384:512, :] = rotate(kb[:LANES], kb[LANES:], tt_ref[256:384, :],
                                                    tt_ref[384:512, :])
    pt_ref[512:768, :] = tr(512, 768).astype(BF16)


def _inproj_kernel(a_scale, b_scale, nc, ctx_ref, x_ref, mod_ref, *refs):
    x = jnp.where(pl.program_id(1) < nc, ctx_ref[...], x_ref[...])
    refs[-1][...] = x
    _inproj_tile(a_scale, b_scale, [x], [slice(None)], mod_ref[...], *refs[:-1])


def _inproj_specs(d, nn, ntr):
    return [pl.BlockSpec((1, d), lambda b, i: (0, 0)),
            pl.BlockSpec((d, nn), lambda b, i: (0, 0), pipeline_mode=pl.Buffered(1)),
            pl.BlockSpec((ntr, d), lambda b, i: (0, 0), pipeline_mode=pl.Buffered(1)),
            pl.BlockSpec((ROW_TILE, 4 * LANES), lambda b, i: (i, 0)),
            pl.BlockSpec((4 * LANES, ROW_TILE), lambda b, i: (0, i))]


def _inproj_outs(bsz, s, nn, ntr):
    return ([pl.BlockSpec((None, ROW_TILE, nn), lambda b, i: (b, i, 0)),
             pl.BlockSpec((None, ntr, ROW_TILE), lambda b, i: (b, 0, i))],
            [jax.ShapeDtypeStruct((bsz, s, nn), BF16), jax.ShapeDtypeStruct((bsz, ntr, s), BF16)])


def _inproj_first(ctx, x, mod, g_pre, w_nat, w_tr, tab_n, tab_t, nc, a_scale, b_scale):
    bsz, t_len, d = x.shape
    s = ctx.shape[1] + t_len
    nt = s // ROW_TILE
    nn, ntr = w_nat.shape[1], w_tr.shape[0]
    msel = lambda b, i: (jnp.where(i < nc, bsz, b), 0, 0)
    out_specs, out_shape = _inproj_outs(bsz, s, nn, ntr)
    return pl.pallas_call(
        functools.partial(_inproj_kernel, a_scale, b_scale, nc),
        grid=(bsz, nt),
        in_specs=[pl.BlockSpec((None, ROW_TILE, d), lambda b, i: (b, jnp.minimum(i, nc - 1), 0)),
                  pl.BlockSpec((None, ROW_TILE, d), lambda b, i: (b, jnp.maximum(i - nc, 0), 0)),
                  pl.BlockSpec((None, 6, d), msel)] + _inproj_specs(d, nn, ntr),
        out_specs=out_specs + [pl.BlockSpec((None, ROW_TILE, d), lambda b, i: (b, i, 0))],
        out_shape=out_shape + [jax.ShapeDtypeStruct((bsz, s, d), F32)],
        compiler_params=_cparams(("parallel", "parallel")),
        name="in_proj",
    )(ctx, x, mod, g_pre, w_nat, w_tr, tab_n, tab_t)


def _attn_kernel(nc, n_ksteps, lam_init, qt_ref, kc_ref, vtc_ref, kl_ref, vtl_ref, lamv_ref, g_ref,
                 o_ref, qz_ref, m_ref, acc_ref, tmax_ref, pv_ref, l_ref, lp_ref):
    qi = pl.program_id(1)
    ki = pl.program_id(2)
    n_groups = 2 * A_HEADS
    dv = vtc_ref.shape[0] // A_HEADS

    def attend(k_ref, vt_ref, skew):
        k = k_ref[...]
        s_q = [_dot(k, qz_ref[g]) for g in range(skew)]
        for g in range(n_groups):
            hd = g // 2
            s = s_q.pop(0)
            if g + skew < n_groups:
                s_q.append(_dot(k, qz_ref[g + skew]))
            m_old = m_ref[g]
            m_new = jnp.maximum(m_old, jnp.max(s, axis=0, keepdims=True))
            p = jnp.exp2(s - m_new)
            alpha = jnp.exp2(m_old - m_new)
            l_ref[g] = alpha * l_ref[g] + jnp.sum(p, axis=0, keepdims=True)
            acc_ref[g] = alpha * acc_ref[g] + _dot(vt_ref[hd * dv:(hd + 1) * dv, :], p.astype(BF16))
            m_ref[g] = m_new

    def attend_lagged(k_ref, vt_ref):
        k = k_ref[...]
        s_q = [_dot(k, qz_ref[g]) for g in range(ATTN_SKEW)]
        excess = None
        for g in range(n_groups):
            hd = g // 2
            s = s_q.pop(0)
            if g + ATTN_SKEW < n_groups:
                s_q.append(_dot(k, qz_ref[g + ATTN_SKEW]))
            m_fix = m_ref[g]
            p = jnp.exp2(s - m_fix)
            t_max = jnp.max(s, axis=0, keepdims=True)
            tmax_ref[g] = t_max
            lp_ref[g] = jnp.sum(p, axis=0, keepdims=True)
            excess = t_max - m_fix if excess is None else jnp.maximum(excess, t_max - m_fix)
            pv_ref[g] = _dot(vt_ref[hd * dv:(hd + 1) * dv, :], p.astype(BF16))
        return jnp.max(excess)

    def commit_lagged():
        for g in range(n_groups):
            m_old = m_ref[g]
            m_new = jnp.maximum(m_old, tmax_ref[g])
            alpha = jnp.exp2(m_old - m_new)
            acc_ref[g] = (acc_ref[g] + pv_ref[g]) * alpha
            l_ref[g] = (l_ref[g] + lp_ref[g]) * alpha
            m_ref[g] = m_new

    @pl.when(ki == 0)
    def _first():
        qt = qt_ref[...]
        rg = (lax.broadcasted_iota(jnp.int32, qt.shape, 0) & (LANES - 1)) >> 4
        for g in range(n_groups):
            qz_ref[g] = jnp.where(rg == g, qt, jnp.zeros_like(qt))
        m_ref[...] = jnp.full(m_ref.shape, -1e30, F32)
        acc_ref[...] = jnp.zeros(acc_ref.shape, F32)
        l_ref[...] = jnp.zeros(l_ref.shape, F32)
        attend(kc_ref, vtc_ref, n_groups)

    @pl.when((ki > 0) & (qi >= nc))
    def _latent():
        within = attend_lagged(kl_ref, vtl_ref) <= ATTN_HEADROOM

        @pl.when(within)
        def _commit():
            commit_lagged()

        @pl.when(jnp.logical_not(within))
        def _redo():
            attend(kl_ref, vtl_ref, ATTN_SKEW)

    @pl.when(ki == n_ksteps - 1)
    def _fin():
        lv = lamv_ref[...]
        a1 = jnp.sum(lv[0:1] * lv[1:2], axis=-1, keepdims=True)
        a2 = jnp.sum(lv[2:3] * lv[3:4], axis=-1, keepdims=True)
        lam = jnp.exp(a1) - jnp.exp(a2) + lam_init
        outs = []
        for hd in range(A_HEADS):
            o = acc_ref[2 * hd] / l_ref[2 * hd] - lam * (acc_ref[2 * hd + 1] / l_ref[2 * hd + 1])
            ms = jnp.mean(o * o, axis=0, keepdims=True)
            outs.append(o * lax.rsqrt(ms + 1e-5) * g_ref[...] * (1.0 - lam_init))
        o_ref[...] = jnp.concatenate(outs, axis=0).T.astype(o_ref.dtype)


def _attention(pn, pt, ctx_len, lamv, g_col, nc, lam_init):
    bsz, s, _ = pn.shape
    t_len = s - ctx_len
    nq = s // ROW_TILE
    tk = min(ATTN_KEY_TILE, t_len)
    assert t_len % tk == 0
    n_ksteps = 1 + t_len // tk
    w = MXU_TILE
    dv_ext = w // A_HEADS
    kl = lambda qi, ki: pl.multiple_of(ctx_len + jnp.where(qi < nc, 0, jnp.maximum(ki - 1, 0)) * tk, ROW_TILE)
    return pl.pallas_call(
        functools.partial(_attn_kernel, nc, n_ksteps, lam_init),
        grid=(bsz, nq, n_ksteps),
        in_specs=[pl.BlockSpec((None, w, ROW_TILE), lambda b, qi, ki: (b, 0, qi)),
                  pl.BlockSpec((None, ctx_len, w), lambda b, qi, ki: (b, 0, 2)),
                  pl.BlockSpec((None, w, ctx_len), lambda b, qi, ki: (b, 2, 0)),
                  pl.BlockSpec((None, pl.Element(tk), pl.Element(w)), lambda b, qi, ki: (b, kl(qi, ki), 2 * w)),
                  pl.BlockSpec((None, pl.Element(w), pl.Element(tk)), lambda b, qi, ki: (b, 2 * w, kl(qi, ki))),
                  pl.BlockSpec(lamv.shape, lambda b, qi, ki: (0, 0)),
                  pl.BlockSpec(g_col.shape, lambda b, qi, ki: (0, 0))],
        out_specs=pl.BlockSpec((None, ROW_TILE, w), lambda b, qi, ki: (b, qi, 0)),
        out_shape=jax.ShapeDtypeStruct((bsz, s, w), BF16),
        scratch_shapes=[pltpu.VMEM((2 * A_HEADS, w, ROW_TILE), BF16),
                        pltpu.VMEM((2 * A_HEADS, 1, ROW_TILE), F32),
                        pltpu.VMEM((2 * A_HEADS, dv_ext, ROW_TILE), F32),
                        pltpu.VMEM((2 * A_HEADS, 1, ROW_TILE), F32),
                        pltpu.VMEM((2 * A_HEADS, dv_ext, ROW_TILE), F32),
                        pltpu.VMEM((2 * A_HEADS, 1, ROW_TILE), F32),
                        pltpu.VMEM((2 * A_HEADS, 1, ROW_TILE), F32)],
        compiler_params=_cparams(("parallel", "parallel", "arbitrary")),
        name="diff_attn",
    )(pt, pn, pt, pn, pt, lamv, g_col)


def _group_norm(y, gmat, eps):
    yh = y.astype(BF16)
    yl = (y - yh.astype(F32)).astype(BF16)
    d = y - (_dot(yh, gmat) + _dot(yl, gmat))
    return d * lax.rsqrt(_dot((d * d).astype(BF16), gmat) + eps)


def _ret_direction(dr, q_ref, kt_ref, v_ref, y_ref, dm_ref, wend_ref, cross_ref, decs_ref, bd_ref,
                   gm_ref, s_ref):
    q = q_ref[...]
    kt = kt_ref[...]
    v = v_ref[...]
    rh = (lax.broadcasted_iota(jnp.int32, kt.shape, 0) & (LANES - 1)) >> 5
    ch = lax.broadcasted_iota(jnp.int32, (q.shape[0], v.shape[1]), 1) >> 6
    y = jnp.zeros((q.shape[0], v.shape[1]), F32)
    for hd in range(B_HEADS):
        ktz = jnp.where(rh == hd, kt, jnp.zeros_like(kt))
        p = (_dot(q, ktz) * dm_ref[dr, hd]).astype(BF16)
        y = jnp.where(ch == hd, _dot(p, v), y)
    s_old = s_ref[dr]
    y = y + _dot(q, s_old.astype(BF16)) * cross_ref[dr]
    kw = (kt.astype(F32) * wend_ref[dr]).astype(BF16)
    s_ref[dr] = decs_ref[...] * s_old + bd_ref[...] * _dot(kw, v)
    y_ref[...] = _group_norm(y, gm_ref[...], 1e-6).astype(y_ref.dtype)


def _fill_window_rows(ext_ref, shift_ref, prev_rows, rows, next_rows):
    r = rows.shape[0]
    ext_ref[0:HALO, :] = prev_rows
    ext_ref[HALO:HALO + r, :] = rows
    ext_ref[HALO + r:, :] = next_rows
    n = shift_ref.shape[1]
    for b in range(1, SUBLANES):
        shift_ref[b - 1] = ext_ref[b:b + n, :]


def _window_rows(ext_ref, shift_ref, o, n):
    a, b = divmod(o, SUBLANES)
    if b == 0:
        return ext_ref[o:o + n, :]
    return shift_ref[b - 1, SUBLANES * a:SUBLANES * a + n, :]


def _pool_mix(i, nc, seg_lens, halo_ok, pool_ref, pool_p, pool_n, wpool_ref, spool_ref, band_ref, yc_ref):
    r, w = pool_ref.shape
    xb = pool_ref[...]
    ext = jnp.concatenate([pool_p[...] * halo_ok[0].astype(BF16), xb, pool_n[...] * halo_ok[1].astype(BF16)],
                          axis=0)
    grp = lax.broadcasted_iota(jnp.int32, (r, w), 1) >> 6
    wsum = _dot(band_ref[0], ext)
    for g in range(1, len(POOL_WINDOWS)):
        wsum = jnp.where(grp == g, _dot(band_ref[g], ext), wsum)
    half = jnp.left_shift(1, grp)
    in_ctx = i < nc
    seg_len = jnp.where(in_ctx, seg_lens[0], seg_lens[1])
    pos = lax.broadcasted_iota(jnp.int32, (r, w), 0) + (i - jnp.where(in_ctx, 0, nc)) * r
    cnt = jnp.minimum(pos + half, seg_len) - jnp.maximum(pos - half, 0)
    dlt = wsum / cnt.astype(F32) - xb.astype(F32)
    yc_ref[...] = (_dot(dlt.astype(BF16), wpool_ref[...]) * spool_ref[...]).astype(yc_ref.dtype)


def _pool_bands(r):
    t = jnp.arange(r)[:, None] + HALO
    j = jnp.arange(r + 2 * HALO)[None, :]
    return jnp.stack([((j >= t - wd // 2) & (j < t + wd - wd // 2)).astype(BF16) for wd in POOL_WINDOWS])


def _conv_mix(halo_ok, glu_ref, glu_p, glu_n, cw_ref, cb_ref, lng_ref, lnb_ref, wpw_ref, yd_ref, ext_ref,
              shift_ref):
    r = glu_ref.shape[0]
    w = ext_ref.shape[1]

    def glu(ref):
        v = ref[...].astype(F32)
        return v[:, :w] * jax.nn.sigmoid(v[:, w:])

    _fill_window_rows(ext_ref, shift_ref, glu(glu_p) * halo_ok[0], glu(glu_ref), glu(glu_n) * halo_ok[1])
    for r0 in range(0, r, CONV_ROWS):
        if r0:
            yield
        acc = jnp.zeros((CONV_ROWS, w), F32) + cb_ref[...]
        for k in range(CONV_K):
            acc = acc + _window_rows(ext_ref, shift_ref, HALO - CONV_K // 2 + k + r0,
                                     CONV_ROWS) * cw_ref[k:k + 1, :]
        mu = jnp.mean(acc, axis=-1, keepdims=True)
        d = acc - mu
        var = jnp.mean(d * d, axis=-1, keepdims=True)
        hn = d * lax.rsqrt(var + 1e-5) * lng_ref[...] + lnb_ref[...]
        yd_ref[r0:r0 + CONV_ROWS, :] = _dot(jax.nn.silu(hn).astype(BF16),
                                           wpw_ref[...]).astype(yd_ref.dtype)


def _bcd_kernel(nc, nt, seg_lens, qf_ref, ktf_ref, vf_ref, qb_ref, ktb_ref, vb_ref, dm_ref, wend_ref,
                cross_ref, decs_ref, bd_ref, gm_ref, glu_ref, glu_p, glu_n, pool_ref, pool_p, pool_n,
                wpool_ref, spool_ref, cw_ref, cb_ref, lng_ref, lnb_ref, wpw_ref, band_ref,
                yf_ref, yb_ref, yc_ref, yd_ref, s_ref, ext_ref, shift_ref):
    i = pl.program_id(1)

    @pl.when(i == 0)
    def _init():
        s_ref[...] = jnp.zeros(s_ref.shape, F32)

    ret_tabs = (dm_ref, wend_ref, cross_ref, decs_ref, bd_ref, gm_ref, s_ref)
    halo_ok = (((i != 0) & (i != nc)).astype(F32), ((i != nc - 1) & (i != nt - 1)).astype(F32))
    _pool_mix(i, nc, seg_lens, halo_ok, pool_ref, pool_p, pool_n, wpool_ref, spool_ref, band_ref, yc_ref)
    conv = _conv_mix(halo_ok, glu_ref, glu_p, glu_n, cw_ref, cb_ref, lng_ref, lnb_ref, wpw_ref, yd_ref, ext_ref,
                     shift_ref)
    _ret_direction(0, qf_ref, ktf_ref, vf_ref, yf_ref, *ret_tabs)
    next(conv, None)
    _ret_direction(1, qb_ref, ktb_ref, vb_ref, yb_ref, *ret_tabs)
    for _ in conv:
        pass


def _mixers_bcd(pn, pt, ret_tabs, cd_weights, nc, seg_lens):
    bsz, s, _ = pn.shape
    c = ROW_TILE
    nt = s // c
    w = MXU_TILE
    cb = lambda i: jnp.where(i < nc, nc - 1 - i, nt - 1 - (i - nc))
    hb = c // HALO
    prev = lambda i: jnp.maximum(i * hb - 1, 0)
    nxt = lambda i: jnp.minimum((i + 1) * hb, s // HALO - 1)
    const = lambda a: pl.BlockSpec(a.shape, lambda b, i: (0,) * a.ndim)
    tile = pl.BlockSpec((None, c, w), lambda b, i: (b, i, 0))
    sds = jax.ShapeDtypeStruct((bsz, s, w), BF16)
    return pl.pallas_call(
        functools.partial(_bcd_kernel, nc, nt, seg_lens),
        grid=(bsz, nt),
        in_specs=[pl.BlockSpec((None, c, w), lambda b, i: (b, i, 3)),
                  pl.BlockSpec((None, w, c), lambda b, i: (b, 1, i)),
                  pl.BlockSpec((None, c, w), lambda b, i: (b, i, 4)),
                  pl.BlockSpec((None, c, w), lambda b, i: (b, cb(i), 3)),
                  pl.BlockSpec((None, w, c), lambda b, i: (b, 1, cb(i))),
                  pl.BlockSpec((None, c, w), lambda b, i: (b, cb(i), 4))]
                 + [const(a) for a in ret_tabs]
                 + [pl.BlockSpec((None, c, 2 * w), lambda b, i: (b, i, 0)),
                    pl.BlockSpec((None, HALO, 2 * w), lambda b, i: (b, prev(i), 0)),
                    pl.BlockSpec((None, HALO, 2 * w), lambda b, i: (b, nxt(i), 0)),
                    pl.BlockSpec((None, c, w), lambda b, i: (b, i, 7)),
                    pl.BlockSpec((None, HALO, w), lambda b, i: (b, prev(i), 7)),
                    pl.BlockSpec((None, HALO, w), lambda b, i: (b, nxt(i), 7))]
                 + [const(a) for a in cd_weights]
                 + [pl.BlockSpec((len(POOL_WINDOWS), c, c + 2 * HALO), lambda b, i: (0, 0, 0))],
        out_specs=[tile, pl.BlockSpec((None, c, w), lambda b, i: (b, cb(i), 0)), tile, tile],
        out_shape=[sds, sds, sds, sds],
        scratch_shapes=[pltpu.VMEM((2, w, w), F32),
                        pltpu.VMEM((c + 2 * HALO, w), F32),
                        pltpu.VMEM((SUBLANES - 1, c + 2 * HALO - SUBLANES, w), F32)],
        compiler_params=_cparams(("parallel", "arbitrary")),
        name="mixers_bcd",
    )(pn, pt, pn, pn, pt, pn, *ret_tabs, pn, pn, pn, pn, pn, pn, *cd_weights, _pool_bands(c))


def _retention_tables(c):
    lg = jnp.asarray([math.log(1.0 - 2.0 ** (-5 - h)) for h in range(B_HEADS)], F32)
    j = jnp.arange(c, dtype=F32)
    dist = j[:, None] - j[None, :]
    df = jnp.where(dist >= 0, jnp.exp(lg[:, None, None] * jnp.maximum(dist, 0.0)), 0.0)
    dmask = jnp.stack([df, jnp.swapaxes(df, 1, 2)])
    row_head = (jnp.arange(MXU_TILE) % LANES) // 32
    col_head = jnp.arange(MXU_TILE) // 64
    wend_f = jnp.exp(lg[row_head][:, None] * (c - 1.0 - j)[None, :])
    wend_b = jnp.exp(lg[row_head][:, None] * j[None, :])
    cross_f = jnp.exp(lg[col_head][None, :] * (j + 1.0)[:, None])
    cross_b = jnp.exp(lg[col_head][None, :] * (c - j)[:, None])
    bdm = (row_head[:, None] == col_head[None, :]).astype(F32)
    decs = bdm * jnp.exp(lg * c)[col_head][None, :]
    gmat = ((col_head[:, None] == col_head[None, :]).astype(F32) / 64.0).astype(BF16)
    return (dmask, jnp.stack([wend_f, wend_b]), jnp.stack([cross_f, cross_b]), decs, bdm, gmat)


def _tail_kernel(next_scales, h_ref, mod_ref, gpm_ref, gprf_ref, gpof_ref, ya_ref, yf_ref, yb_ref, gf_ref,
                 gb_ref, yc_ref, yd_ref, wo_ref, wg_ref, wu_ref, wd_ref, *rest):
    o_ref = rest[0] if next_scales is None else rest[6]
    rb = h_ref.shape[0] // TAIL_SUBBLOCKS
    rows = [pl.ds(j * rb, rb) for j in range(TAIL_SUBBLOCKS)]

    def mix_in(rs):
        yb = (jax.nn.silu(gf_ref[rs, :].astype(F32)) * yf_ref[rs, :].astype(F32)
              + jax.nn.silu(gb_ref[rs, :].astype(F32)) * yb_ref[rs, :].astype(F32))
        ycat = jnp.concatenate([ya_ref[rs, :], yb.astype(BF16), yc_ref[rs, :], yd_ref[rs, :]], axis=-1)
        return _dot(ycat, wo_ref[...])

    ys = [mix_in(rs) for rs in rows]
    xs, us = [], []
    for rs, y in zip(rows, ys):
        x = h_ref[rs, :] + mod_ref[2:3, :] * (_rms(y, 1e-6) * gpm_ref[...])
        u = _rms(x, 1e-6) * gprf_ref[...]
        xs.append(x)
        us.append((u * (1.0 + mod_ref[4:5, :]) + mod_ref[3:4, :]).astype(BF16))
    acts = [(jax.nn.silu(_dot(ub, wg_ref[...])) * _dot(ub, wu_ref[...])).astype(BF16) for ub in us]
    fs = [_dot(a, wd_ref[...]) for a in acts]
    outs = [x + mod_ref[5:6, :] * (_rms(f, 1e-6) * gpof_ref[...]) for x, f in zip(xs, fs)]
    for rs, o in zip(rows, outs):
        o_ref[rs, :] = o
    if next_scales is not None:
        _inproj_tile(*next_scales, outs, rows, rest[0][...], *rest[1:6], *rest[7:9])


def _tail(h, mod, g_post_mix, g_pre_ffn, g_post_ffn, ya, yf, yb, pn, yc, yd, w_out, wg, wu, wd, nc,
          latent_only, next_inproj=None):
    bsz, s, d = h.shape
    r = ROW_TILE
    nt = s // r
    out_rows = s - nc * r if latent_only else s
    out_tile = (lambda b, i: (b, jnp.maximum(i - nc, 0), 0)) if latent_only else (lambda b, i: (b, i, 0))
    w = MXU_TILE
    msel = lambda b, i: (jnp.where(i < nc, bsz, b), 0, 0)
    tile = lambda col: pl.BlockSpec((None, r, w), lambda b, i: (b, i, col))
    vec = pl.BlockSpec((1, d), lambda b, i: (0, 0))
    resident = lambda a: pl.BlockSpec(a.shape, lambda b, i: (0, 0), pipeline_mode=pl.Buffered(1))
    in_specs = [pl.BlockSpec((None, r, d), lambda b, i: (b, i, 0)),
                pl.BlockSpec((None, 6, d), msel),
                vec, vec, vec,
                tile(0), tile(0), tile(0), tile(5), tile(6), tile(0), tile(0),
                resident(w_out), resident(wg), resident(wu), resident(wd)]
    args = [h, mod, g_post_mix, g_pre_ffn, g_post_ffn, ya, yf, yb, pn, pn, yc, yd, w_out, wg, wu, wd]
    out_specs = [pl.BlockSpec((None, r, d), out_tile)]
    out_shape = [jax.ShapeDtypeStruct((bsz, out_rows, d), F32)]
    next_scales = None
    if next_inproj is not None:
        next_scales = tuple(next_inproj[6:8])
        nn, ntr = next_inproj[2].shape[1], next_inproj[3].shape[0]
        in_specs += [pl.BlockSpec((None, 6, d), msel)] + _inproj_specs(d, nn, ntr)
        args += list(next_inproj[:6])
        pn_specs, pn_shapes = _inproj_outs(bsz, s, nn, ntr)
        out_specs += pn_specs
        out_shape += pn_shapes
    res = pl.pallas_call(
        functools.partial(_tail_kernel, next_scales),
        grid=(bsz, nt),
        in_specs=in_specs,
        out_specs=out_specs,
        out_shape=out_shape,
        compiler_params=_cparams(("parallel", "arbitrary")),
        name="out_ffn",
    )(*args)
    return res[0] if next_inproj is None else res


def _rope_tables(t_len, ctx_len, dqk_a, dk_b):
    rows = t_len // GRID_W
    row = jnp.repeat(jnp.arange(rows, dtype=F32), GRID_W)
    col = jnp.tile(jnp.arange(GRID_W, dtype=F32), rows)

    def cs(d, reps):
        n_freq = d // 4
        inv = ROPE_BASE ** (-jnp.arange(n_freq, dtype=F32) / n_freq)
        ang = jnp.concatenate([row[:, None] * inv, col[:, None] * inv], axis=-1)
        cos = jnp.concatenate([jnp.ones((ctx_len, d // 2), F32), jnp.cos(ang)], axis=0)
        sin = jnp.concatenate([jnp.zeros((ctx_len, d // 2), F32), jnp.sin(ang)], axis=0)
        return jnp.tile(cos, (1, reps)), jnp.tile(sin, (1, reps))

    ca, sa = cs(dqk_a, LANES // (dqk_a // 2))
    cb, sb = cs(dk_b, LANES // (dk_b // 2))
    tab_n = jnp.concatenate([ca, sa, cb, sb], axis=1)
    return tab_n, tab_n.T


def _split_rotary_halves(w, n_groups):
    lead, n = w.shape[:-1], w.shape[-1]
    return jnp.swapaxes(w.reshape(*lead, n_groups, 2, n // (2 * n_groups)), -3, -2).reshape(*lead, n)


def kernel(x, c, ctx, c_ctx, w_ada, b_ada, g_pre_mix, g_post_mix, g_pre_ffn, g_post_ffn, w_in, w_out, lam_q1, lam_k1, lam_q2, lam_k2, g_subln, w_pool, s_pool, conv_w, conv_b, conv_ln_g, conv_ln_b, w_conv_out, w_ffn_gate, w_ffn_up, w_ffn_down):
    bsz, t_len, d = x.shape
    ctx_len = ctx.shape[1]
    depth = w_in.shape[0]
    assert d == 8 * A_HEADS * 32 and t_len % ROW_TILE == 0 and ctx_len % ROW_TILE == 0
    assert t_len % GRID_W == 0 and bsz < 8
    qw = d // 4
    dqk_a = qw // (2 * A_HEADS)
    dk_b = qw // B_HEADS
    nc = ctx_len // ROW_TILE

    cond = jnp.zeros((8, d), F32).at[:bsz].set(c).at[bsz].set(c_ctx)
    mods = _modulation(cond, w_ada, b_ada).reshape(depth, 8, 6, d)

    col = lambda k: w_in[:, :, k * qw:(k + 1) * qw]
    k_a, v_a, k_b, v_b, q_a, q_b, g_f, g_b, pool = (col(k) for k in range(9))
    glu = w_in[:, :, 9 * qw:]
    w_nat = jnp.concatenate([glu, _split_rotary_halves(k_a, 2 * A_HEADS), _split_rotary_halves(q_b, B_HEADS), v_b, g_f, g_b, pool], axis=-1).astype(BF16)
    w_tr = jnp.swapaxes(jnp.concatenate([_split_rotary_halves(q_a, 2 * A_HEADS), _split_rotary_halves(k_b, B_HEADS), v_a], axis=-1), 1, 2).astype(BF16)
    w_out_b = w_out.astype(BF16)
    wg_b, wu_b, wd_b = w_ffn_gate.astype(BF16), w_ffn_up.astype(BF16), w_ffn_down.astype(BF16)
    w_pw_b = w_conv_out.astype(BF16)
    eye = jnp.eye(len(POOL_WINDOWS), dtype=F32)
    wpool_bd = jnp.einsum("lgce,gh->lgche", w_pool, eye).reshape(depth, qw, qw).astype(BF16)
    conv_w2 = jnp.pad(conv_w.reshape(depth, CONV_K, qw), ((0, 0), (0, 1), (0, 0)))
    lamv = jnp.stack([lam_q1, lam_k1, lam_q2, lam_k2], axis=1)
    g_col = g_subln[:, :, None]

    tab_n, tab_t = _rope_tables(t_len, ctx_len, dqk_a, dk_b)
    ret_tabs = _retention_tables(ROW_TILE)
    row = lambda a, l: a[l][None, :]

    inproj_args = lambda l: (mods[l], row(g_pre_mix, l), w_nat[l], w_tr[l], tab_n, tab_t,
                             dqk_a ** -0.5 * LOG2_E, dk_b ** -0.5)
    m0, g0, wn0, wt0, _, _, a_scale, b_scale = inproj_args(0)
    pn, pt, h = _inproj_first(ctx, x, m0, g0, wn0, wt0, tab_n, tab_t, nc, a_scale, b_scale)
    for l in range(depth):
        lam_init = 0.8 - 0.6 * math.exp(-0.3 * l)
        mod = mods[l]
        ya = _attention(pn, pt, ctx_len, lamv[l], g_col[l], nc, lam_init)
        cd_weights = (wpool_bd[l], row(s_pool, l), conv_w2[l], row(conv_b, l), row(conv_ln_g, l),
                      row(conv_ln_b, l), w_pw_b[l])
        yf, yb, yc, yd = _mixers_bcd(pn, pt, ret_tabs, cd_weights, nc, (ctx_len, t_len))
        last = l == depth - 1
        res = _tail(h, mod, row(g_post_mix, l), row(g_pre_ffn, l), row(g_post_ffn, l), ya, yf, yb, pn,
                    yc, yd, w_out_b[l], wg_b[l], wu_b[l], wd_b[l], nc, latent_only=last,
                    next_inproj=None if last else inproj_args(l + 1))
        h, pn, pt = (res, None, None) if last else res
    return h
```
